```python
import math
import jax, jax.numpy as jnp
from jax import lax
import numpy as np

D_MODEL = 2048
BATCH = 4
SEQ = 4096
DEPTH = 2

N_MIXERS = 2
HEAD_DIM = 128
N_HEADS = D_MODEL // (2 * HEAD_DIM)
V_DIM = 2 * HEAD_DIM
QKV_DIM = 2 * N_HEADS * HEAD_DIM * 2 + N_HEADS * V_DIM
ROT_DIM = HEAD_DIM // 4
ROPE_THETA = 500000.0
Q_BLOCK = 128
SSM_GROUP = 16
N_GROUPS = D_MODEL // SSM_GROUP
SSM_STATE = 64
SSM_CHUNK = 128
D_FF = 256 * ((8 * D_MODEL // 3 + 255) // 256)
N_SUBLAYERS = 3
N_ATTN_LAYERS = (DEPTH + 1) // 2
N_SSM_LAYERS = DEPTH // 2
NORM_EPS = 1e-6
MACARON_WEIGHT = 0.5

kernel_name = "hybrid_diffattn_s5_macaron_adaln"


def rms_norm(x, g):
    xf = x.astype(jnp.float32)
    y = xf * lax.rsqrt(jnp.mean(xf * xf, axis=-1, keepdims=True) + NORM_EPS)
    return (y * g.astype(jnp.float32)).astype(x.dtype)


def rope_tables(positions):
    inv_freq = ROPE_THETA ** (-jnp.arange(0, ROT_DIM, 2, dtype=jnp.float32) / ROT_DIM)
    ang = positions.astype(jnp.float32)[..., None] * inv_freq
    return jnp.cos(ang), jnp.sin(ang)


def apply_partial_rope(t, cos, sin):
    cos = cos[:, None, None]
    sin = sin[:, None, None]
    tf = t.astype(jnp.float32)
    half = ROT_DIM // 2
    r1 = tf[..., :half]
    r2 = tf[..., half:ROT_DIM]
    out = jnp.concatenate([r1 * cos - r2 * sin, r2 * cos + r1 * sin, tf[..., ROT_DIM:]], axis=-1)
    return out.astype(t.dtype)


def swiglu(h, w_in, w_out):
    a, b = jnp.split(h @ w_in, 2, axis=-1)
    return (jax.nn.silu(a) * b) @ w_out


def diff_attention(h, w_in, w_out, q_g, k_g, lam_vec, subln_g, cos, sin, lambda_init):
    bsz, seq, _ = h.shape
    qkv = h @ w_in
    qd = 2 * N_HEADS * HEAD_DIM
    q, k, v = jnp.split(qkv, [qd, 2 * qd], axis=-1)
    q = q.reshape(bsz, seq, N_HEADS, 2, HEAD_DIM).transpose(0, 2, 3, 1, 4)
    k = k.reshape(bsz, seq, N_HEADS, 2, HEAD_DIM).transpose(0, 2, 3, 1, 4)
    v = v.reshape(bsz, seq, N_HEADS, V_DIM).transpose(0, 2, 1, 3)
    q = apply_partial_rope(rms_norm(q, q_g), cos, sin)
    k = apply_partial_rope(rms_norm(k, k_g), cos, sin)
    lv = lam_vec.astype(jnp.float32)
    lam = jnp.exp(jnp.sum(lv[0] * lv[1])) - jnp.exp(jnp.sum(lv[2] * lv[3])) + lambda_init
    scale = HEAD_DIM ** -0.5
    n_blocks = seq // Q_BLOCK
    q_blocks = q.reshape(bsz, N_HEADS, 2, n_blocks, Q_BLOCK, HEAD_DIM).transpose(3, 0, 1, 2, 4, 5)
    k_pos = jnp.arange(seq)

    def one_block(args):
        q_blk, blk = args
        s = jnp.einsum('bhcqd,bhckd->bhcqk', q_blk, k, preferred_element_type=jnp.float32) * scale
        q_pos = blk * Q_BLOCK + jnp.arange(Q_BLOCK)
        mask = k_pos[None, :] <= q_pos[:, None]
        p = jax.nn.softmax(jnp.where(mask, s, -jnp.inf), axis=-1)
        attn = p[:, :, 0] - lam * p[:, :, 1]
        return jnp.einsum('bhqk,bhkv->bhqv', attn.astype(v.dtype), v)

    o = lax.map(one_block, (q_blocks, jnp.arange(n_blocks)))
    o = o.transpose(1, 2, 0, 3, 4).reshape(bsz, N_HEADS, seq, V_DIM)
    o = rms_norm(o, subln_g) * (1.0 - lambda_init)
    o = o.transpose(0, 2, 1, 3).reshape(bsz, seq, N_HEADS * V_DIM)
    return o @ w_out


def _ssm_combine(e1, e2):
    a1r, a1i, b1r, b1i = e1
    a2r, a2i, b2r, b2i = e2
    return (a2r * a1r - a2i * a1i,
            a2r * a1i + a2i * a1r,
            a2r * b1r - a2i * b1i + b2r,
            a2r * b1i + a2i * b1r + b2i)


def s5_glu_mixer(h, a_re, a_im, log_step, b_re, b_im, c_re, c_im, d_skip, w_glu):
    bsz, seq, _ = h.shape
    f32 = jnp.float32
    u = h.astype(f32).reshape(bsz, seq, N_GROUPS, SSM_GROUP)
    ar = a_re.astype(f32)
    ai = a_im.astype(f32)
    dt = jnp.exp(log_step.astype(f32))[:, None]
    mag = jnp.exp(dt * ar)
    abar_re = mag * jnp.cos(dt * ai)
    abar_im = mag * jnp.sin(dt * ai)
    den = ar * ar + ai * ai
    num_re = abar_re - 1.0
    coef_re = (num_re * ar + abar_im * ai) / den
    coef_im = (abar_im * ar - num_re * ai) / den
    br = b_re.astype(f32)
    bi = b_im.astype(f32)
    bbar_re = coef_re[..., None] * br - coef_im[..., None] * bi
    bbar_im = coef_re[..., None] * bi + coef_im[..., None] * br
    cr = c_re.astype(f32)
    ci = c_im.astype(f32)
    n_chunks = seq // SSM_CHUNK
    u_chunks = u.reshape(bsz, n_chunks, SSM_CHUNK, N_GROUPS, SSM_GROUP).transpose(1, 2, 0, 3, 4)
    a_shape = (SSM_CHUNK, bsz, N_GROUPS, SSM_STATE)
    abar_re_b = jnp.broadcast_to(abar_re, a_shape)
    abar_im_b = jnp.broadcast_to(abar_im, a_shape)

    def chunk_step(carry, u_c):
        h_re, h_im = carry
        bu_re = jnp.einsum('tbgp,gnp->tbgn', u_c, bbar_re)
        bu_im = jnp.einsum('tbgp,gnp->tbgn', u_c, bbar_im)
        p_re, p_im, l_re, l_im = lax.associative_scan(
            _ssm_combine, (abar_re_b, abar_im_b, bu_re, bu_im), axis=0)
        s_re = l_re + p_re * h_re - p_im * h_im
        s_im = l_im + p_re * h_im + p_im * h_re
        y_c = jnp.einsum('tbgn,gpn->tbgp', s_re, cr) - jnp.einsum('tbgn,gpn->tbgp', s_im, ci)
        return (s_re[-1], s_im[-1]), y_c

    zeros = jnp.zeros((bsz, N_GROUPS, SSM_STATE), f32)
    _, y = lax.scan(chunk_step, (zeros, zeros), u_chunks)
    y = y.transpose(2, 0, 1, 3, 4).reshape(bsz, seq, D_MODEL)
    y = y + d_skip.astype(f32) * h.astype(f32)
    z = jax.nn.gelu(y).astype(h.dtype)
    val, gate = jnp.split(z @ w_glu, 2, axis=-1)
    return val * jax.nn.sigmoid(gate)


def setup_inputs(seed: int = 0) -> dict:
    key = jax.random.key(seed)
    ks = jax.random.split(key, 24)
    f32 = jnp.float32
    D, F = D_MODEL, D_FF
    nrm = lambda k, shape, s: jax.random.normal(k, shape, f32) * s
    x = jax.random.normal(ks[0], (BATCH, SEQ, D), f32)
    c = jax.random.normal(ks[1], (BATCH, D), f32)
    offsets = jax.random.randint(ks[2], (BATCH, 1), 0, 1024, dtype=jnp.int32)
    positions = offsets + jnp.arange(SEQ, dtype=jnp.int32)[None, :]
    norm_g = 1.0 + nrm(ks[3], (DEPTH, N_SUBLAYERS, D), 0.02)
    ada_w = nrm(ks[4], (DEPTH, D, N_SUBLAYERS * 3 * D), 0.5 * D ** -0.5)
    ada_b = nrm(ks[5], (DEPTH, N_SUBLAYERS * 3 * D), 0.02)
    ffn_w_in = nrm(ks[6], (DEPTH, 2, D, 2 * F), D ** -0.5)
    ffn_w_out = nrm(ks[7], (DEPTH, 2, F, D), F ** -0.5)
    attn_w_in = nrm(ks[8], (N_ATTN_LAYERS, D, QKV_DIM), D ** -0.5)
    attn_w_out = nrm(ks[9], (N_ATTN_LAYERS, N_HEADS * V_DIM, D), (N_HEADS * V_DIM) ** -0.5)
    attn_q_norm = 1.0 + nrm(ks[10], (N_ATTN_LAYERS, HEAD_DIM), 0.02)
    attn_k_norm = 1.0 + nrm(ks[11], (N_ATTN_LAYERS, HEAD_DIM), 0.02)
    attn_lambda = nrm(ks[12], (N_ATTN_LAYERS, 4, HEAD_DIM), 0.1)
    attn_subln = 1.0 + nrm(ks[13], (N_ATTN_LAYERS, V_DIM), 0.02)
    ssm_a_re = -0.5 + nrm(ks[14], (N_SSM_LAYERS, N_GROUPS, SSM_STATE), 0.01)
    ssm_a_im = jnp.broadcast_to(math.pi * jnp.arange(SSM_STATE, dtype=f32),
                                (N_SSM_LAYERS, N_GROUPS, SSM_STATE))
    ssm_log_step = jax.random.uniform(ks[15], (N_SSM_LAYERS, N_GROUPS), f32,
                                      math.log(1e-3), math.log(1e-1))
    b_s = (0.5 / SSM_GROUP) ** 0.5
    c_s = (0.5 / SSM_STATE) ** 0.5
    ssm_b_re = nrm(ks[16], (N_SSM_LAYERS, N_GROUPS, SSM_STATE, SSM_GROUP), b_s)
    ssm_b_im = nrm(ks[17], (N_SSM_LAYERS, N_GROUPS, SSM_STATE, SSM_GROUP), b_s)
    ssm_c_re = nrm(ks[18], (N_SSM_LAYERS, N_GROUPS, SSM_GROUP, SSM_STATE), c_s)
    ssm_c_im = nrm(ks[19], (N_SSM_LAYERS, N_GROUPS, SSM_GROUP, SSM_STATE), c_s)
    ssm_d = nrm(ks[20], (N_SSM_LAYERS, D), 1.0)
    ssm_w_glu = nrm(ks[21], (N_SSM_LAYERS, D, 2 * D), D ** -0.5)
    return {"x": x, "c": c, "positions": positions, "norm_g": norm_g,
            "ada_w": ada_w, "ada_b": ada_b, "ffn_w_in": ffn_w_in, "ffn_w_out": ffn_w_out,
            "attn_w_in": attn_w_in, "attn_w_out": attn_w_out, "attn_q_norm": attn_q_norm,
            "attn_k_norm": attn_k_norm, "attn_lambda": attn_lambda, "attn_subln": attn_subln,
            "ssm_a_re": ssm_a_re, "ssm_a_im": ssm_a_im, "ssm_log_step": ssm_log_step,
            "ssm_b_re": ssm_b_re, "ssm_b_im": ssm_b_im, "ssm_c_re": ssm_c_re,
            "ssm_c_im": ssm_c_im, "ssm_d": ssm_d, "ssm_w_glu": ssm_w_glu}


def reference(x, c, positions, norm_g, ada_w, ada_b, ffn_w_in, ffn_w_out,
              attn_w_in, attn_w_out, attn_q_norm, attn_k_norm, attn_lambda, attn_subln,
              ssm_a_re, ssm_a_im, ssm_log_step, ssm_b_re, ssm_b_im, ssm_c_re,
              ssm_c_im, ssm_d, ssm_w_glu):
    bsz = x.shape[0]
    cos, sin = rope_tables(positions)
    cond = jax.nn.silu(c)
    for i in range(DEPTH):
        mod = (cond @ ada_w[i] + ada_b[i]).reshape(bsz, N_SUBLAYERS, 3, D_MODEL)
        shift = mod[:, :, 0][:, :, None, :]
        scale = mod[:, :, 1][:, :, None, :]
        gate = mod[:, :, 2][:, :, None, :]

        h = rms_norm(x, norm_g[i, 0]) * (1.0 + scale[:, 0]) + shift[:, 0]
        x = x + MACARON_WEIGHT * gate[:, 0] * swiglu(h, ffn_w_in[i, 0], ffn_w_out[i, 0])

        h = rms_norm(x, norm_g[i, 1]) * (1.0 + scale[:, 1]) + shift[:, 1]
        j = i // N_MIXERS
        if i % N_MIXERS == 0:
            lambda_init = 0.8 - 0.6 * math.exp(-0.3 * i)
            m = diff_attention(h, attn_w_in[j], attn_w_out[j], attn_q_norm[j], attn_k_norm[j],
                               attn_lambda[j], attn_subln[j], cos, sin, lambda_init)
        else:
            m = s5_glu_mixer(h, ssm_a_re[j], ssm_a_im[j], ssm_log_step[j], ssm_b_re[j],
                             ssm_b_im[j], ssm_c_re[j], ssm_c_im[j], ssm_d[j], ssm_w_glu[j])
        x = x + gate[:, 1] * m

        h = rms_norm(x, norm_g[i, 2]) * (1.0 + scale[:, 2]) + shift[:, 2]
        x = x + MACARON_WEIGHT * gate[:, 2] * swiglu(h, ffn_w_in[i, 1], ffn_w_out[i, 1])
    return x
```

```python
import functools
import math

import numpy as np
import jax
import jax.numpy as jnp
from jax import lax
from jax.experimental import pallas as pl
from jax.experimental.pallas import tpu as pltpu

HEAD_DIM = 128
V_DIM = 2 * HEAD_DIM
ROT_DIM = HEAD_DIM // 4
ROPE_THETA = 500000.0
SSM_GROUP = 16
SSM_STATE = 64
N_SUBLAYERS = 3
N_MIXERS = 2
NORM_EPS = 1e-6
MACARON_WEIGHT = 0.5

V7X_LANES = 128
V7X_SUBLANES = 8
V7X_MXU_DIM = 256
V7X_VMEM_BYTES = 64 * 1024 * 1024

F32 = jnp.float32
BF16 = jnp.bfloat16
NEG_BIG = -1e30

SSM_GROUPS_PER_BLOCK = V7X_MXU_DIM // SSM_GROUP
SSM_BLOCK_STATES = SSM_GROUPS_PER_BLOCK * SSM_STATE
SSM_SCAN_LANES = 512


def _compiler_params(n_axes, vmem_bytes):
    limit = min(int(vmem_bytes), V7X_VMEM_BYTES - 4 * 1024 * 1024)
    return pltpu.CompilerParams(dimension_semantics=("arbitrary",) * n_axes,
                                vmem_limit_bytes=limit)


def _row_map(layout, nt, ncol, col_of_j):
    if layout == "bs":
        return lambda b, ti, j: (b * nt + ti, col_of_j(j))
    return lambda b, ti, j: (ti, b * ncol + col_of_j(j))


def _stream_shape(layout, B, S, D):
    return (B * S, D) if layout == "bs" else (S, B * D)


def _zero(j):
    return 0


def _ident(j):
    return j


def _norm_mod(x, g, shift, scale):
    ms = jnp.mean(x * x, axis=-1, keepdims=True)
    y = x * lax.rsqrt(ms + NORM_EPS) * g
    return y * (1.0 + scale) + shift


def _adaln_body(c_ref, w_ref, b_ref, o_ref):
    c = c_ref[...]
    cond = (c * jax.nn.sigmoid(c)).astype(BF16)
    o_ref[...] = jnp.dot(cond, w_ref[...].astype(BF16), preferred_element_type=F32) + b_ref[...]


def _adaln(c, ada_w, ada_b):
    L, D, N = ada_w.shape
    B = c.shape[0]
    rows = V7X_SUBLANES * ((B + V7X_SUBLANES - 1) // V7X_SUBLANES)
    c_pad = jnp.pad(c, ((0, rows - B), (0, 0)))
    tn = D
    out = pl.pallas_call(
        _adaln_body,
        out_shape=jax.ShapeDtypeStruct((L, rows, N), F32),
        grid=(L, N // tn),
        in_specs=[
            pl.BlockSpec((rows, D), lambda l, j: (0, 0)),
            pl.BlockSpec((None, D, tn), lambda l, j: (l, 0, j)),
            pl.BlockSpec((None, 1, tn), lambda l, j: (l, 0, j)),
        ],
        out_specs=pl.BlockSpec((None, rows, tn), lambda l, j: (l, 0, j)),
        compiler_params=_compiler_params(2, 2 * D * tn * 4 + D * tn * 2 + (8 << 20)),
        name="adaln",
    )(c_pad, ada_w, ada_b.reshape(L, 1, N))
    return out[:, :B].reshape(L, B, N_SUBLAYERS * 3, D)


def _ffn_body(x_ref, mod_ref, g_ref, wa_ref, wb_ref, wo_ref, o_ref, h_scr, *, sl):
    j = pl.program_id(2)
    nj = pl.num_programs(2)

    @pl.when(j == 0)
    def _():
        h = _norm_mod(x_ref[...], g_ref[...], mod_ref[3 * sl:3 * sl + 1, :],
                      mod_ref[3 * sl + 1:3 * sl + 2, :])
        h_scr[...] = h.astype(BF16)

    h = h_scr[...]
    a = jnp.dot(h, wa_ref[...], preferred_element_type=F32)
    b = jnp.dot(h, wb_ref[...], preferred_element_type=F32)
    act = (a * jax.nn.sigmoid(a) * b).astype(BF16)
    contrib = jnp.dot(act, wo_ref[...], preferred_element_type=F32)

    @pl.when(j == 0)
    def _():
        o_ref[...] = contrib

    @pl.when(j > 0)
    def _():
        o_ref[...] += contrib

    @pl.when(j == nj - 1)
    def _():
        gate = mod_ref[3 * sl + 2:3 * sl + 3, :]
        o_ref[...] = x_ref[...] + (MACARON_WEIGHT * gate) * o_ref[...]


def _ffn(xv, mod, norm_g, w_in, w_out, *, layer, sl, which, B, S, lay_in, lay_out, tm, tf):
    D = norm_g.shape[-1]
    F = w_out.shape[2]
    tm = min(tm, S)
    nt, nj = S // tm, F // tf
    vmem = (4 * tm * D * 4) + tm * D * 2 + 2 * 3 * D * tf * 2 + 3 * tm * tf * 4 + tm * D * 4 + (4 << 20)
    return pl.pallas_call(
        functools.partial(_ffn_body, sl=sl),
        out_shape=jax.ShapeDtypeStruct(_stream_shape(lay_out, B, S, D), F32),
        grid=(B, nt, nj),
        in_specs=[
            pl.BlockSpec((tm, D), _row_map(lay_in, nt, 1, _zero)),
            pl.BlockSpec((None, None, N_SUBLAYERS * 3, D), lambda b, ti, j: (layer, b, 0, 0)),
            pl.BlockSpec((None, None, 1, D), lambda b, ti, j: (layer, sl, 0, 0)),
            pl.BlockSpec((None, None, D, tf), lambda b, ti, j: (layer, which, 0, j)),
            pl.BlockSpec((None, None, D, tf), lambda b, ti, j: (layer, which, 0, nj + j)),
            pl.BlockSpec((None, None, tf, D), lambda b, ti, j: (layer, which, j, 0)),
        ],
        out_specs=pl.BlockSpec((tm, D), _row_map(lay_out, nt, 1, _zero)),
        scratch_shapes=[pltpu.VMEM((tm, D), BF16)],
        compiler_params=_compiler_params(3, vmem),
        name=f"ffn_l{layer}_s{sl}",
    )(xv, mod, norm_g.reshape(norm_g.shape[0], N_SUBLAYERS, 1, D), w_in, w_in, w_out)


def _qkv_body(x_ref, pos_ref, invf_ref, mod_ref, g_ref, w_ref, gain_ref, o_ref,
              h_scr, cos_scr, sa_scr, sb_scr, *, n_q_tiles, q_mult):
    jn = pl.program_id(2)
    half = ROT_DIM // 2

    @pl.when(jn == 0)
    def _():
        h = _norm_mod(x_ref[...], g_ref[...], mod_ref[3:4, :], mod_ref[4:5, :])
        h_scr[...] = h.astype(BF16)
        ang = pos_ref[...].astype(F32) * invf_ref[...]
        c, s = jnp.cos(ang), jnp.sin(ang)
        lane = lax.broadcasted_iota(jnp.int32, ang.shape, 1)
        cos_scr[...] = jnp.where(lane < ROT_DIM, c, 1.0)
        sa_scr[...] = jnp.where(lane < half, -s, 0.0)
        sb_scr[...] = jnp.where((lane >= half) & (lane < ROT_DIM), s, 0.0)

    acc = jnp.dot(h_scr[...], w_ref[...], preferred_element_type=F32)
    tn = acc.shape[-1]

    @pl.when(jn < 2 * n_q_tiles)
    def _():
        mult = jnp.where(jn < n_q_tiles, q_mult, 1.0).astype(F32)
        gain = gain_ref[...] * mult
        cos, sa, sb = cos_scr[...], sa_scr[...], sb_scr[...]
        for c in range(tn // HEAD_DIM):
            t = acc[:, c * HEAD_DIM:(c + 1) * HEAD_DIM]
            ms = jnp.mean(t * t, axis=-1, keepdims=True)
            y = t * lax.rsqrt(ms + NORM_EPS) * gain
            r = (y * cos + pltpu.roll(y, HEAD_DIM - half, 1) * sa + pltpu.roll(y, half, 1) * sb)
            o_ref[:, c * HEAD_DIM:(c + 1) * HEAD_DIM] = r.astype(BF16)

    @pl.when(jn >= 2 * n_q_tiles)
    def _():
        o_ref[...] = acc.astype(BF16)


def _qkv(xv, pos, mod, norm_g, w, q_gain, k_gain, *, layer, j, B, S, tm, tn):
    D = norm_g.shape[-1]
    N = w.shape[-1]
    tm = min(tm, S)
    nt, nn = S // tm, N // tn
    n_q_tiles = D // tn
    half = ROT_DIM // 2
    inv_freq = ROPE_THETA ** (-np.arange(0, ROT_DIM, 2, dtype=np.float64) / ROT_DIM)
    invf = jnp.asarray(np.tile(inv_freq, HEAD_DIM // half)[None, :], dtype=F32)
    gains = jnp.stack([q_gain, k_gain], axis=1).reshape(-1, 2, 1, HEAD_DIM)
    q_mult = HEAD_DIM ** -0.5 * math.log2(math.e)
    vmem = 2 * tm * D * 4 + tm * D * 2 + 2 * D * tn * 2 + 2 * tm * tn * 2 + 2 * tm * tn * 4 \
        + 5 * tm * V7X_LANES * 4 + (4 << 20)
    return pl.pallas_call(
        functools.partial(_qkv_body, n_q_tiles=n_q_tiles, q_mult=q_mult),
        out_shape=jax.ShapeDtypeStruct((B * S, N), BF16),
        grid=(B, nt, nn),
        in_specs=[
            pl.BlockSpec((tm, D), _row_map("bs", nt, 1, _zero)),
            pl.BlockSpec((tm, 1), _row_map("bs", nt, 1, _zero)),
            pl.BlockSpec((1, HEAD_DIM), lambda b, ti, jn: (0, 0)),
            pl.BlockSpec((None, None, N_SUBLAYERS * 3, D), lambda b, ti, jn: (layer, b, 0, 0)),
            pl.BlockSpec((None, None, 1, D), lambda b, ti, jn: (layer, 1, 0, 0)),
            pl.BlockSpec((None, D, tn), lambda b, ti, jn: (j, 0, jn)),
            pl.BlockSpec((None, None, 1, HEAD_DIM),
                         lambda b, ti, jn: (j, jnp.minimum(jn // n_q_tiles, 1), 0, 0)),
        ],
        out_specs=pl.BlockSpec((tm, tn), _row_map("bs", nt, 1, _ident)),
        scratch_shapes=[pltpu.VMEM((tm, D), BF16)] + [pltpu.VMEM((tm, HEAD_DIM), F32)] * 3,
        compiler_params=_compiler_params(3, vmem),
        name=f"qkv_l{layer}",
    )(xv, pos, invf, mod, norm_g.reshape(norm_g.shape[0], N_SUBLAYERS, 1, D), w, gains)


def _attn_body(q_ref, k_ref, v_ref, lam_ref, sub_ref, o_ref, m_scr, l_scr, acc_scr,
               *, tq, lambda_init):
    qi = pl.program_id(2)
    m_scr[...] = jnp.full(m_scr.shape, NEG_BIG, F32)
    l_scr[...] = jnp.zeros(l_scr.shape, F32)
    acc_scr[...] = jnp.zeros(acc_scr.shape, F32)
    q = q_ref[...]
    row = lax.broadcasted_iota(jnp.int32, (tq, tq), 0)
    col = lax.broadcasted_iota(jnp.int32, (tq, tq), 1)
    causal = row >= col

    def block(kb, masked):
        ks = pl.multiple_of(kb * tq, tq)
        kblk = k_ref[pl.ds(ks, tq), :]
        vblk = v_ref[pl.ds(ks, tq), :]
        for c in range(2):
            qc = q[:, c * HEAD_DIM:(c + 1) * HEAD_DIM]
            kc = kblk[:, c * HEAD_DIM:(c + 1) * HEAD_DIM]
            s = lax.dot_general(qc, kc, (((1,), (1,)), ((), ())), preferred_element_type=F32)
            if masked:
                s = jnp.where(causal, s, NEG_BIG)
            m_prev = m_scr[c]
            m_new = jnp.maximum(m_prev, jnp.max(s, axis=-1, keepdims=True))
            p = jnp.exp2(s - m_new)
            alpha = jnp.exp2(m_prev - m_new)
            l_scr[c] = alpha * l_scr[c] + jnp.sum(p, axis=-1, keepdims=True)
            acc_scr[c] = alpha * acc_scr[c] + jnp.dot(p.astype(BF16), vblk,
                                                      preferred_element_type=F32)
            m_scr[c] = m_new

    def body(kb, carry):
        block(kb, False)
        return carry

    lax.fori_loop(0, qi, body, 0)
    block(qi, True)

    lv = lam_ref[...]
    lam = (jnp.exp(jnp.sum(lv[0:1] * lv[1:2], axis=-1, keepdims=True))
           - jnp.exp(jnp.sum(lv[2:3] * lv[3:4], axis=-1, keepdims=True)) + lambda_init)
    o = acc_scr[0] / l_scr[0] - lam * (acc_scr[1] / l_scr[1])
    ms = jnp.mean(o * o, axis=-1, keepdims=True)
    o = (o * lax.rsqrt(ms + NORM_EPS) * sub_ref[...]) * (1.0 - lambda_init)
    o_ref[...] = o.astype(BF16)


def _attn(qkv, lam, subln, *, j, B, S, D, tq, lambda_init):
    H = D // V_DIM
    tq = min(tq, S)
    nq = S // tq
    vmem = 2 * 2 * S * V_DIM * 2 + 4 * tq * V_DIM * 2 + 2 * tq * V_DIM * 4 \
        + 4 * tq * V7X_LANES * 4 + 6 * tq * tq * 4 + (4 << 20)
    return pl.pallas_call(
        functools.partial(_attn_body, tq=tq, lambda_init=lambda_init),
        out_shape=jax.ShapeDtypeStruct((B * S, D), BF16),
        grid=(B, H, nq),
        in_specs=[
            pl.BlockSpec((tq, V_DIM), lambda b, h, qi: (b * nq + qi, h)),
            pl.BlockSpec((S, V_DIM), lambda b, h, qi: (b, H + h)),
            pl.BlockSpec((S, V_DIM), lambda b, h, qi: (b, 2 * H + h)),
            pl.BlockSpec((None, 4, HEAD_DIM), lambda b, h, qi: (j, 0, 0)),
            pl.BlockSpec((None, 1, V_DIM), lambda b, h, qi: (j, 0, 0)),
        ],
        out_specs=pl.BlockSpec((tq, V_DIM), lambda b, h, qi: (b * nq + qi, h)),
        scratch_shapes=[pltpu.VMEM((2, tq, 1), F32), pltpu.VMEM((2, tq, 1), F32),
                        pltpu.VMEM((2, tq, V_DIM), F32)],
        compiler_params=_compiler_params(3, vmem),
        name="diff_attn",
    )(qkv, qkv, qkv, lam, subln.reshape(-1, 1, V_DIM))


def _oproj_body(x_ref, a_ref, w_ref, mod_ref, o_ref):
    m = jnp.dot(a_ref[...], w_ref[...], preferred_element_type=F32)
    o_ref[...] = x_ref[...] + mod_ref[5:6, :] * m


def _oproj(xv, a, w, mod, *, layer, j, B, S, tm):
    D = xv.shape[-1]
    tm = min(tm, S)
    nt = S // tm
    vmem = 4 * tm * D * 4 + 2 * tm * D * 2 + 2 * D * D * 2 + tm * D * 4 + (4 << 20)
    return pl.pallas_call(
        _oproj_body,
        out_shape=jax.ShapeDtypeStruct((B * S, D), F32),
        grid=(B, nt, 1),
        in_specs=[
            pl.BlockSpec((tm, D), _row_map("bs", nt, 1, _zero)),
            pl.BlockSpec((tm, D), _row_map("bs", nt, 1, _zero)),
            pl.BlockSpec((None, D, D), lambda b, ti, jn: (j, 0, 0)),
            pl.BlockSpec((None, None, N_SUBLAYERS * 3, D), lambda b, ti, jn: (layer, b, 0, 0)),
        ],
        out_specs=pl.BlockSpec((tm, D), _row_map("bs", nt, 1, _zero)),
        compiler_params=_compiler_params(3, vmem),
        name=f"attn_out_l{layer}",
    )(xv, a, w, mod)


def _ssm_prep_body(ar_ref, ai_ref, ls_ref, arr_ref, air_ref, lsr_ref, br_ref, bi_ref,
                   abr_ref, abi_ref, a2r_ref, a2i_ref, bbr_ref, bbi_ref):
    def abar(ar, ai, ls):
        dt = jnp.exp(ls)
        mag = jnp.exp(dt * ar)
        return mag * jnp.cos(dt * ai), mag * jnp.sin(dt * ai)

    re, im = abar(ar_ref[...], ai_ref[...], ls_ref[...])
    abr_ref[...] = re
    abi_ref[...] = im
    a2r_ref[...] = re * re - im * im
    a2i_ref[...] = 2.0 * (re * im)

    ar, ai = arr_ref[...], air_ref[...]
    re, im = abar(ar, ai, lsr_ref[...])
    den = ar * ar + ai * ai
    num_re = re - 1.0
    coef_re = (num_re * ar + im * ai) / den
    coef_im = (im * ar - num_re * ai) / den
    br, bi = br_ref[...], bi_ref[...]
    bbr_ref[...] = coef_re * br - coef_im * bi
    bbi_ref[...] = coef_re * bi + coef_im * br


def _ssm_prep(a_re, a_im, log_step, b_re, b_im):
    G, N = a_re.shape
    P = b_re.shape[-1]
    rep = lambda v: jnp.repeat(v, P, axis=0)
    flat = lambda v: jnp.swapaxes(v, 1, 2).reshape(G * P, N)
    ls = log_step.reshape(G, 1)
    small = jax.ShapeDtypeStruct((G, N), F32)
    big = jax.ShapeDtypeStruct((G * P, N), F32)
    return pl.pallas_call(
        _ssm_prep_body,
        out_shape=(small, small, small, small, big, big),
        name="ssm_discretise",
    )(a_re, a_im, ls, rep(a_re), rep(a_im), rep(ls), flat(b_re), flat(b_im))


def _ssm_block_operands(abr, abi, a2r, a2i, bbr, bbi, c_re, c_im):
    G, N = abr.shape
    P = SSM_GROUP
    gb = SSM_GROUPS_PER_BLOCK
    nb = G // gb
    eye = jnp.eye(gb, dtype=F32)

    def b_block(v):
        v = v.reshape(nb, gb, P, 1, N) * eye[None, :, None, :, None]
        return v.reshape(nb, gb * P, gb * N)

    def c_block(v):
        v = jnp.swapaxes(v.reshape(nb, gb, P, N), 2, 3)
        v = v.reshape(nb, gb, N, 1, P) * eye[None, :, None, :, None]
        return v.reshape(nb, gb * N, gb * P)

    bblk = jnp.concatenate([b_block(bbr), b_block(bbi)], axis=-1).astype(BF16)
    cblk = jnp.concatenate([c_block(c_re), -c_block(c_im)], axis=1).astype(BF16)

    first = (lax.broadcasted_iota(jnp.int32, (1, 1, V7X_SUBLANES, 1), 2) < V7X_SUBLANES // 2)
    a1 = jnp.stack([abr, abi], axis=0).reshape(2, nb, 1, gb * N).swapaxes(0, 1)
    a2 = jnp.stack([a2r, a2i], axis=0).reshape(2, nb, 1, gb * N).swapaxes(0, 1)
    a_second = jnp.where(first, 0.0, a1)
    a_carry = jnp.where(first, a1, a2)
    return bblk, cblk, a_second, a_carry


def _gelu_tanh(x):
    c = math.sqrt(2.0 / math.pi)
    return x * (0.5 * (1.0 + jnp.tanh(c * (x + 0.044715 * (x * x * x)))))


def _ssm_body(x_ref, sh_ref, sc_ref, g_ref, d_ref, bblk_ref, cblk_ref, asec_ref, acar_ref, z_ref,
              hf_scr, hb_scr, bu_scr, y_scr, carry_scr, *, tm, nb):
    i = pl.program_id(0)
    D = x_ref.shape[-1]
    sub = V7X_SUBLANES
    cin = SSM_GROUPS_PER_BLOCK * SSM_GROUP
    ns = SSM_BLOCK_STATES

    @pl.when(i == 0)
    def _():
        carry_scr[...] = jnp.zeros(carry_scr.shape, F32)

    x = x_ref[...]
    ms = jnp.mean(x * x, axis=-1, keepdims=True)
    y = (x * lax.rsqrt(ms + NORM_EPS) * g_ref[...]).reshape(tm // sub, sub, D)
    h = (y * (1.0 + sc_ref[...])[None] + sh_ref[...][None]).reshape(tm, D)
    hf_scr[...] = h
    hb_scr[...] = h.astype(BF16)

    first = lax.broadcasted_iota(jnp.int32, (sub, SSM_SCAN_LANES), 0) < sub // 2

    for gb in range(nb):
        bu_scr[...] = jnp.dot(hb_scr[:, gb * cin:(gb + 1) * cin], bblk_ref[gb],
                              preferred_element_type=F32)
        for part in range(ns // SSM_SCAN_LANES):
            lo = part * SSM_SCAN_LANES
            re_sl = slice(lo, lo + SSM_SCAN_LANES)
            im_sl = slice(ns + lo, ns + lo + SSM_SCAN_LANES)
            mr, mi = asec_ref[gb, 0, :, re_sl], asec_ref[gb, 1, :, re_sl]
            ar, ai = acar_ref[gb, 0, :, re_sl], acar_ref[gb, 1, :, re_sl]

            def step(r, carry):
                hr, hi = carry
                rows = pl.ds(pl.multiple_of(r * sub, sub), sub)
                vr, vi = bu_scr[rows, re_sl], bu_scr[rows, im_sl]
                vr2, vi2 = pltpu.roll(vr, sub // 2, 0), pltpu.roll(vi, sub // 2, 0)
                sr = vr + (vr2 * mr - vi2 * mi) + (hr * ar - hi * ai)
                si = vi + (vr2 * mi + vi2 * mr) + (hr * ai + hi * ar)
                bu_scr[rows, re_sl] = sr
                bu_scr[rows, im_sl] = si
                return (jnp.where(first, pltpu.roll(sr, sub // 2, 0), sr),
                        jnp.where(first, pltpu.roll(si, sub // 2, 0), si))

            hr, hi = lax.fori_loop(0, tm // sub, step,
                                   (carry_scr[gb, 0, :, re_sl], carry_scr[gb, 1, :, re_sl]),
                                   unroll=2)
            carry_scr[gb, 0, :, re_sl] = hr
            carry_scr[gb, 1, :, re_sl] = hi
        y_scr[:, gb * cin:(gb + 1) * cin] = jnp.dot(bu_scr[...].astype(BF16), cblk_ref[gb],
                                                    preferred_element_type=F32)

    z_ref[...] = _gelu_tanh(y_scr[...] + d_ref[...] * hf_scr[...]).astype(BF16)


def _ssm(xi, mod, norm_g, d_skip, bblk, cblk, a_second, a_carry, *, layer, j, B, S, tm):
    D = norm_g.shape[-1]
    assert V7X_SUBLANES % B == 0 and V7X_SUBLANES // B == 2, "scan packs 2 timesteps x B batches per row-group"
    nb = bblk.shape[0]
    rows = S * B
    tm = min(tm, rows)
    shift8 = jnp.tile(mod[layer, :, 3, :], (V7X_SUBLANES // B, 1))
    scale8 = jnp.tile(mod[layer, :, 4, :], (V7X_SUBLANES // B, 1))
    ns2 = 2 * SSM_BLOCK_STATES
    const = lambda *shape: pl.BlockSpec(shape, lambda i: (0,) * len(shape))
    vmem = 2 * tm * D * 4 + tm * D * 4 + tm * D * 2 + tm * ns2 * 4 + tm * D * 4 + 2 * tm * D * 2 \
        + 2 * (bblk.size + cblk.size) * 2 + tm * ns2 * 2 + tm * D * 4 + (6 << 20)
    return pl.pallas_call(
        functools.partial(_ssm_body, tm=tm, nb=nb),
        out_shape=jax.ShapeDtypeStruct((rows, D), BF16),
        grid=(rows // tm,),
        in_specs=[
            pl.BlockSpec((tm, D), lambda i: (i, 0)),
            const(V7X_SUBLANES, D),
            const(V7X_SUBLANES, D),
            pl.BlockSpec((None, None, 1, D), lambda i: (layer, 1, 0, 0)),
            pl.BlockSpec((None, 1, D), lambda i: (j, 0, 0)),
            const(*bblk.shape),
            const(*cblk.shape),
            const(*a_second.shape),
            const(*a_carry.shape),
        ],
        out_specs=pl.BlockSpec((tm, D), lambda i: (i, 0)),
        scratch_shapes=[
            pltpu.VMEM((tm, D), F32),
            pltpu.VMEM((tm, D), BF16),
            pltpu.VMEM((tm, ns2), F32),
            pltpu.VMEM((tm, D), F32),
            pltpu.VMEM((nb, 2, V7X_SUBLANES, SSM_BLOCK_STATES), F32),
        ],
        compiler_params=_compiler_params(1, vmem),
        name=f"ssm_l{layer}",
    )(xi, shift8, scale8, norm_g.reshape(norm_g.shape[0], N_SUBLAYERS, 1, D),
      d_skip.reshape(-1, 1, D), bblk, cblk, a_second, a_carry)


def _glu_body(x_ref, z_ref, wv_ref, wg_ref, mod_ref, o_ref):
    z = z_ref[...]
    val = jnp.dot(z, wv_ref[...], preferred_element_type=F32)
    gt = jnp.dot(z, wg_ref[...], preferred_element_type=F32)
    o_ref[...] = x_ref[...] + mod_ref[5:6, :] * (val * jax.nn.sigmoid(gt))


def _glu(xv, zv, w, mod, *, layer, j, B, S, tm, tn):
    D = w.shape[1]
    tm = min(tm, S)
    nt, nn = S // tm, D // tn
    vmem = 4 * tm * tn * 4 + 2 * tm * D * 2 + 4 * D * tn * 2 + 3 * tm * tn * 4 + (4 << 20)
    return pl.pallas_call(
        _glu_body,
        out_shape=jax.ShapeDtypeStruct((S, B * D), F32),
        grid=(B, nt, nn),
        in_specs=[
            pl.BlockSpec((tm, tn), _row_map("sb", nt, nn, _ident)),
            pl.BlockSpec((tm, D), _row_map("sb", nt, 1, _zero)),
            pl.BlockSpec((None, D, tn), lambda b, ti, jn: (j, 0, jn)),
            pl.BlockSpec((None, D, tn), lambda b, ti, jn: (j, 0, nn + jn)),
            pl.BlockSpec((None, None, N_SUBLAYERS * 3, tn), lambda b, ti, jn: (layer, b, 0, jn)),
        ],
        out_specs=pl.BlockSpec((tm, tn), _row_map("sb", nt, nn, _ident)),
        compiler_params=_compiler_params(3, vmem),
        name=f"glu_l{layer}",
    )(xv, zv, w, w, mod)


def kernel(x, c, positions, norm_g, ada_w, ada_b, ffn_w_in, ffn_w_out, attn_w_in, attn_w_out,
           attn_q_norm, attn_k_norm, attn_lambda, attn_subln, ssm_a_re, ssm_a_im, ssm_log_step,
           ssm_b_re, ssm_b_im, ssm_c_re, ssm_c_im, ssm_d, ssm_w_glu):
    B, S, D = x.shape
    depth = norm_g.shape[0]

    mod = _adaln(c, ada_w, ada_b)
    ffn_w_in = ffn_w_in.astype(BF16)
    ffn_w_out = ffn_w_out.astype(BF16)
    attn_w_in = attn_w_in.astype(BF16)
    attn_w_out = attn_w_out.astype(BF16)
    ssm_w_glu = ssm_w_glu.astype(BF16)
    pos = positions.reshape(B * S, 1)

    ffn = functools.partial(_ffn, mod=mod, norm_g=norm_g, w_in=ffn_w_in, w_out=ffn_w_out,
                            B=B, S=S, tm=512, tf=512)
    layout = "bs"
    xv = x.reshape(B * S, D)
    for i in range(depth):
        mixer_layout = "bs" if i % N_MIXERS == 0 else "sb"
        j = i // N_MIXERS
        xv = ffn(xv, layer=i, sl=0, which=0, lay_in=layout, lay_out=mixer_layout)
        layout = mixer_layout
        if i % N_MIXERS == 0:
            lambda_init = 0.8 - 0.6 * math.exp(-0.3 * i)
            qkv = _qkv(xv, pos, mod, norm_g, attn_w_in, attn_q_norm, attn_k_norm,
                       layer=i, j=j, B=B, S=S, tm=1024, tn=1024)
            att = _attn(qkv, attn_lambda, attn_subln, j=j, B=B, S=S, D=D, tq=512,
                        lambda_init=lambda_init)
            xv = _oproj(xv, att, attn_w_out, mod, layer=i, j=j, B=B, S=S, tm=512)
        else:
            disc = _ssm_prep(ssm_a_re[j], ssm_a_im[j], ssm_log_step[j], ssm_b_re[j], ssm_b_im[j])
            operands = _ssm_block_operands(*disc, ssm_c_re[j], ssm_c_im[j])
            z = _ssm(xv.reshape(S * B, D), mod, norm_g, ssm_d, *operands,
                     layer=i, j=j, B=B, S=S, tm=256)
            xv = _glu(xv, z.reshape(S, B * D), ssm_w_glu, mod, layer=i, j=j, B=B, S=S,
                      tm=512, tn=1024)
        next_layout = "bs" if (i + 1) % N_MIXERS == 0 or i + 1 == depth else "sb"
        if i + 1 == depth:
            next_layout = "bs"
        xv = ffn(xv, layer=i, sl=2, which=1, lay_in=layout, lay_out=next_layout)
        layout = next_layout
    return xv.reshape(B, S, D)
```

```python
import functools
import math

import numpy as np
import jax
import jax.numpy as jnp
from jax import lax
from jax.experimental import pallas as pl
from jax.experimental.pallas import tpu as pltpu

HEAD_DIM = 128
V_DIM = 2 * HEAD_DIM
ROT_DIM = HEAD_DIM // 4
ROPE_THETA = 500000.0
SSM_GROUP = 16
SSM_STATE = 64
N_SUBLAYERS = 3
N_MIXERS = 2
NORM_EPS = 1e-6
MACARON_WEIGHT = 0.5

V7X_LANES = 128
V7X_SUBLANES = 8
V7X_MXU_DIM = 256
V7X_VMEM_BYTES = 64 * 1024 * 1024

F32 = jnp.float32
BF16 = jnp.bfloat16
NEG_BIG = -1e30

SSM_GROUPS_PER_BLOCK = V7X_MXU_DIM // SSM_GROUP
SSM_BLOCK_STATES = SSM_GROUPS_PER_BLOCK * SSM_STATE
SSM_SCAN_LANES = 512


def _compiler_params(n_axes, vmem_bytes):
    limit = min(int(vmem_bytes), V7X_VMEM_BYTES - 4 * 1024 * 1024)
    return pltpu.CompilerParams(dimension_semantics=("arbitrary",) * n_axes,
                                vmem_limit_bytes=limit)


def _resident(block_shape, index_map):
    return pl.BlockSpec(block_shape, index_map, pipeline_mode=pl.Buffered(1))


def _norm_mod(x, g, shift, scale):
    ms = jnp.mean(x * x, axis=-1, keepdims=True)
    y = x * lax.rsqrt(ms + NORM_EPS) * g
    return y * (1.0 + scale) + shift


def _adaln_body(c_ref, w_ref, b_ref, o_ref):
    c = c_ref[...]
    cond = (c * jax.nn.sigmoid(c)).astype(BF16)
    o_ref[...] = jnp.dot(cond, w_ref[...].astype(BF16), preferred_element_type=F32) + b_ref[...]


def _adaln(c, ada_w, ada_b):
    L, D, N = ada_w.shape
    B = c.shape[0]
    rows = V7X_SUBLANES * ((B + V7X_SUBLANES - 1) // V7X_SUBLANES)
    c_pad = jnp.pad(c, ((0, rows - B), (0, 0)))
    tn = D
    out = pl.pallas_call(
        _adaln_body,
        out_shape=jax.ShapeDtypeStruct((L, rows, N), F32),
        grid=(L, N // tn),
        in_specs=[
            pl.BlockSpec((rows, D), lambda l, j: (0, 0)),
            pl.BlockSpec((None, D, tn), lambda l, j: (l, 0, j)),
            pl.BlockSpec((None, 1, tn), lambda l, j: (l, 0, j)),
        ],
        out_specs=pl.BlockSpec((None, rows, tn), lambda l, j: (l, 0, j)),
        compiler_params=_compiler_params(2, 2 * D * tn * 4 + D * tn * 2 + (8 << 20)),
        name="adaln",
    )(c_pad, ada_w, ada_b.reshape(L, 1, N))
    return out[:, :B].reshape(L, B, N_SUBLAYERS * 3, D)


def _ffn_body(x_ref, mod_ref, g_ref, wa_ref, wb_ref, wo_ref, o_ref, h_scr, *, sl):
    j = pl.program_id(2)
    nj = pl.num_programs(2)

    @pl.when(j == 0)
    def _():
        h = _norm_mod(x_ref[...], g_ref[...], mod_ref[3 * sl:3 * sl + 1, :],
                      mod_ref[3 * sl + 1:3 * sl + 2, :])
        h_scr[...] = h.astype(BF16)
        o_ref[...] = jnp.zeros(o_ref.shape, F32)

    h = h_scr[...]
    a = jnp.dot(h, wa_ref[...], preferred_element_type=F32)
    b = jnp.dot(h, wb_ref[...], preferred_element_type=F32)
    act = (a * jax.nn.sigmoid(a) * b).astype(BF16)
    o_ref[...] += jnp.dot(act, wo_ref[...], preferred_element_type=F32)

    @pl.when(j == nj - 1)
    def _():
        gate = mod_ref[3 * sl + 2:3 * sl + 3, :]
        o_ref[...] = x_ref[...] + (MACARON_WEIGHT * gate) * o_ref[...]


def _ffn(xv, mod, norm_g, w_in, w_out, *, layer, sl, which, B, S, tm, tf):
    D = norm_g.shape[-1]
    F = w_out.shape[2]
    tm = min(tm, S)
    nt, nj = S // tm, F // tf
    rows = lambda b, ti, j: (b * nt + ti, 0)
    vmem = 3 * tm * D * 4 + tm * D * 2 + 2 * 3 * D * tf * 2 + 4 * tm * tf * 4 + (6 << 20)
    return pl.pallas_call(
        functools.partial(_ffn_body, sl=sl),
        out_shape=jax.ShapeDtypeStruct((B * S, D), F32),
        grid=(B, nt, nj),
        in_specs=[
            _resident((tm, D), rows),
            pl.BlockSpec((None, None, N_SUBLAYERS * 3, D), lambda b, ti, j: (layer, b, 0, 0)),
            pl.BlockSpec((None, None, 1, D), lambda b, ti, j: (layer, sl, 0, 0)),
            pl.BlockSpec((None, None, D, tf), lambda b, ti, j: (layer, which, 0, j)),
            pl.BlockSpec((None, None, D, tf), lambda b, ti, j: (layer, which, 0, nj + j)),
            pl.BlockSpec((None, None, tf, D), lambda b, ti, j: (layer, which, j, 0)),
        ],
        out_specs=pl.BlockSpec((tm, D), rows),
        scratch_shapes=[pltpu.VMEM((tm, D), BF16)],
        compiler_params=_compiler_params(3, vmem),
        name=f"ffn_l{layer}_s{sl}",
    )(xv, mod, norm_g.reshape(norm_g.shape[0], N_SUBLAYERS, 1, D), w_in, w_in, w_out)


def _qkv_body(x_ref, pos_ref, invf_ref, mod_ref, g_ref, w_ref, gain_ref, o_ref, *, tn, q_mult):
    D = x_ref.shape[-1]
    half = ROT_DIM // 2
    h = _norm_mod(x_ref[...], g_ref[...], mod_ref[3:4, :], mod_ref[4:5, :]).astype(BF16)

    ang = pos_ref[...].astype(F32) * invf_ref[...]
    c, s = jnp.cos(ang), jnp.sin(ang)
    lane = lax.broadcasted_iota(jnp.int32, ang.shape, 1)
    cos = jnp.where(lane < ROT_DIM, c, 1.0)
    sin_a = jnp.where(lane < half, -s, 0.0)
    sin_b = jnp.where((lane >= half) & (lane < ROT_DIM), s, 0.0)
    gains = (gain_ref[0] * q_mult, gain_ref[1])

    for jn in range(w_ref.shape[-1] // tn):
        cols = slice(jn * tn, (jn + 1) * tn)
        acc = jnp.dot(h, w_ref[:, cols], preferred_element_type=F32)
        which = (jn * tn) // D
        if which == 2:
            o_ref[:, cols] = acc.astype(BF16)
            continue
        for ch in range(tn // HEAD_DIM):
            t = acc[:, ch * HEAD_DIM:(ch + 1) * HEAD_DIM]
            ms = jnp.mean(t * t, axis=-1, keepdims=True)
            y = t * lax.rsqrt(ms + NORM_EPS) * gains[which]
            r = (y * cos + pltpu.roll(y, HEAD_DIM - half, 1) * sin_a
                 + pltpu.roll(y, half, 1) * sin_b)
            lo = jn * tn + ch * HEAD_DIM
            o_ref[:, lo:lo + HEAD_DIM] = r.astype(BF16)


def _qkv(xv, pos, mod, norm_g, w, q_gain, k_gain, *, layer, j, B, S, tm, tn):
    D = norm_g.shape[-1]
    N = w.shape[-1]
    tm = min(tm, S)
    nt = S // tm
    half = ROT_DIM // 2
    inv_freq = ROPE_THETA ** (-np.arange(0, ROT_DIM, 2, dtype=np.float64) / ROT_DIM)
    invf = jnp.asarray(np.tile(inv_freq, HEAD_DIM // half)[None, :], dtype=F32)
    gains = jnp.stack([q_gain, k_gain], axis=1).reshape(-1, 2, 1, HEAD_DIM)
    q_mult = HEAD_DIM ** -0.5 * math.log2(math.e)
    rows = lambda b, ti: (b * nt + ti, 0)
    vmem = 2 * tm * D * 4 + tm * D * 2 + D * N * 2 + 2 * tm * N * 2 + 4 * tm * tn * 4 \
        + 8 * tm * V7X_LANES * 4 + (6 << 20)
    return pl.pallas_call(
        functools.partial(_qkv_body, tn=tn, q_mult=q_mult),
        out_shape=jax.ShapeDtypeStruct((B * S, N), BF16),
        grid=(B, nt),
        in_specs=[
            pl.BlockSpec((tm, D), rows),
            pl.BlockSpec((tm, 1), rows),
            pl.BlockSpec((1, HEAD_DIM), lambda b, ti: (0, 0)),
            pl.BlockSpec((None, None, N_SUBLAYERS * 3, D), lambda b, ti: (layer, b, 0, 0)),
            pl.BlockSpec((None, None, 1, D), lambda b, ti: (layer, 1, 0, 0)),
            _resident((None, D, N), lambda b, ti: (j, 0, 0)),
            pl.BlockSpec((None, 2, 1, HEAD_DIM), lambda b, ti: (j, 0, 0, 0)),
        ],
        out_specs=pl.BlockSpec((tm, N), rows),
        compiler_params=_compiler_params(2, vmem),
        name=f"qkv_l{layer}",
    )(xv, pos, invf, mod, norm_g.reshape(norm_g.shape[0], N_SUBLAYERS, 1, D), w, gains)


def _attn_body(q_ref, k_ref, v_ref, lam_ref, sub_ref, o_ref, m_scr, l_scr, acc_scr,
               *, tq, lambda_init):
    qi = pl.program_id(2)
    m_scr[...] = jnp.full(m_scr.shape, NEG_BIG, F32)
    l_scr[...] = jnp.zeros(l_scr.shape, F32)
    acc_scr[...] = jnp.zeros(acc_scr.shape, F32)
    q = q_ref[...]
    key = lax.broadcasted_iota(jnp.int32, (tq, tq), 0)
    qry = lax.broadcasted_iota(jnp.int32, (tq, tq), 1)
    causal = key <= qry

    def block(kb, masked):
        ks = pl.multiple_of(kb * tq, tq)
        kblk = k_ref[pl.ds(ks, tq), :]
        vblk = v_ref[pl.ds(ks, tq), :]
        for c in range(2):
            qc = q[:, c * HEAD_DIM:(c + 1) * HEAD_DIM]
            kc = kblk[:, c * HEAD_DIM:(c + 1) * HEAD_DIM]
            s = lax.dot_general(kc, qc, (((1,), (1,)), ((), ())), preferred_element_type=F32)
            if masked:
                s = jnp.where(causal, s, NEG_BIG)
            m_prev = m_scr[c]
            m_new = jnp.maximum(m_prev, jnp.max(s, axis=0, keepdims=True))
            p = jnp.exp2(s - m_new)
            alpha = jnp.exp2(m_prev - m_new)
            l_scr[c] = alpha * l_scr[c] + jnp.sum(p, axis=0, keepdims=True)
            pv = lax.dot_general(vblk, p.astype(BF16), (((0,), (0,)), ((), ())),
                                 preferred_element_type=F32)
            acc_scr[c] = alpha * acc_scr[c] + pv
            m_scr[c] = m_new

    def body(kb, carry):
        block(kb, False)
        return carry

    lax.fori_loop(0, qi, body, 0)
    block(qi, True)

    lv = lam_ref[...]
    lam = (jnp.exp(jnp.sum(lv[0:1] * lv[1:2], axis=-1, keepdims=True))
           - jnp.exp(jnp.sum(lv[2:3] * lv[3:4], axis=-1, keepdims=True)) + lambda_init)
    o = acc_scr[0] / l_scr[0] - lam * (acc_scr[1] / l_scr[1])
    ms = jnp.mean(o * o, axis=0, keepdims=True)
    o = (o * lax.rsqrt(ms + NORM_EPS)).T
    o_ref[...] = ((o * sub_ref[...]) * (1.0 - lambda_init)).astype(BF16)


def _attn(qkv, lam, subln, *, j, B, S, D, tq, lambda_init):
    H = D // V_DIM
    tq = min(tq, S)
    nq = S // tq
    vmem = 2 * 2 * S * V_DIM * 2 + 4 * tq * V_DIM * 2 + 2 * tq * V_DIM * 4 \
        + 4 * tq * V7X_SUBLANES * 4 + 8 * tq * tq * 4 + (6 << 20)
    return pl.pallas_call(
        functools.partial(_attn_body, tq=tq, lambda_init=lambda_init),
        out_shape=jax.ShapeDtypeStruct((B * S, D), BF16),
        grid=(B, H, nq),
        in_specs=[
            pl.BlockSpec((tq, V_DIM), lambda b, h, qi: (b * nq + qi, h)),
            pl.BlockSpec((S, V_DIM), lambda b, h, qi: (b, H + h)),
            pl.BlockSpec((S, V_DIM), lambda b, h, qi: (b, 2 * H + h)),
            pl.BlockSpec((None, 4, HEAD_DIM), lambda b, h, qi: (j, 0, 0)),
            pl.BlockSpec((None, 1, V_DIM), lambda b, h, qi: (j, 0, 0)),
        ],
        out_specs=pl.BlockSpec((tq, V_DIM), lambda b, h, qi: (b * nq + qi, h)),
        scratch_shapes=[pltpu.VMEM((2, 1, tq), F32), pltpu.VMEM((2, 1, tq), F32),
                        pltpu.VMEM((2, V_DIM, tq), F32)],
        compiler_params=_compiler_params(3, vmem),
        name="diff_attn",
    )(qkv, qkv, qkv, lam, subln.reshape(-1, 1, V_DIM))


def _oproj_body(x_ref, a_ref, w_ref, mod_ref, o_ref):
    m = jnp.dot(a_ref[...], w_ref[...], preferred_element_type=F32)
    o_ref[...] = x_ref[...] + mod_ref[5:6, :] * m


def _oproj(xv, a, w, mod, *, layer, j, B, S, tm):
    D = xv.shape[-1]
    tm = min(tm, S)
    nt = S // tm
    rows = lambda b, ti: (b * nt + ti, 0)
    vmem = 4 * tm * D * 4 + 2 * tm * D * 2 + D * D * 2 + tm * D * 4 + (6 << 20)
    return pl.pallas_call(
        _oproj_body,
        out_shape=jax.ShapeDtypeStruct((B * S, D), F32),
        grid=(B, nt),
        in_specs=[
            pl.BlockSpec((tm, D), rows),
            pl.BlockSpec((tm, D), rows),
            _resident((None, D, D), lambda b, ti: (j, 0, 0)),
            pl.BlockSpec((None, None, N_SUBLAYERS * 3, D), lambda b, ti: (layer, b, 0, 0)),
        ],
        out_specs=pl.BlockSpec((tm, D), rows),
        compiler_params=_compiler_params(2, vmem),
        name=f"attn_out_l{layer}",
    )(xv, a, w, mod)


def _ssm_prep_body(ar_ref, ai_ref, ls_ref, arr_ref, air_ref, lsr_ref, br_ref, bi_ref,
                   abr_ref, abi_ref, a2r_ref, a2i_ref, bbr_ref, bbi_ref):
    def abar(ar, ai, ls):
        dt = jnp.exp(ls)
        mag = jnp.exp(dt * ar)
        return mag * jnp.cos(dt * ai), mag * jnp.sin(dt * ai)

    re, im = abar(ar_ref[...], ai_ref[...], ls_ref[...])
    abr_ref[...] = re
    abi_ref[...] = im
    a2r_ref[...] = re * re - im * im
    a2i_ref[...] = 2.0 * (re * im)

    ar, ai = arr_ref[...], air_ref[...]
    re, im = abar(ar, ai, lsr_ref[...])
    den = ar * ar + ai * ai
    num_re = re - 1.0
    coef_re = (num_re * ar + im * ai) / den
    coef_im = (im * ar - num_re * ai) / den
    br, bi = br_ref[...], bi_ref[...]
    bbr_ref[...] = coef_re * br - coef_im * bi
    bbi_ref[...] = coef_re * bi + coef_im * br


def _ssm_prep(a_re, a_im, log_step, b_re, b_im):
    G, N = a_re.shape
    P = b_re.shape[-1]
    rep = lambda v: jnp.repeat(v, P, axis=0)
    flat = lambda v: jnp.swapaxes(v, 1, 2).reshape(G * P, N)
    ls = log_step.reshape(G, 1)
    small = jax.ShapeDtypeStruct((G, N), F32)
    big = jax.ShapeDtypeStruct((G * P, N), F32)
    return pl.pallas_call(
        _ssm_prep_body,
        out_shape=(small, small, small, small, big, big),
        name="ssm_discretise",
    )(a_re, a_im, ls, rep(a_re), rep(a_im), rep(ls), flat(b_re), flat(b_im))


def _ssm_block_operands(abr, abi, a2r, a2i, bbr, bbi, c_re, c_im):
    G, N = abr.shape
    P = SSM_GROUP
    gb = SSM_GROUPS_PER_BLOCK
    nb = G // gb
    eye = jnp.eye(gb, dtype=F32)

    def b_block(v):
        v = v.reshape(nb, gb, P, 1, N) * eye[None, :, None, :, None]
        return v.reshape(nb, gb * P, gb * N)

    def c_block(v):
        v = jnp.swapaxes(v.reshape(nb, gb, P, N), 2, 3)
        v = v.reshape(nb, gb, N, 1, P) * eye[None, :, None, :, None]
        return v.reshape(nb, gb * N, gb * P)

    bblk = jnp.concatenate([b_block(bbr), b_block(bbi)], axis=-1).astype(BF16)
    cblk = jnp.concatenate([c_block(c_re), -c_block(c_im)], axis=1).astype(BF16)

    first = (lax.broadcasted_iota(jnp.int32, (1, 1, V7X_SUBLANES, 1), 2) < V7X_SUBLANES // 2)
    a1 = jnp.stack([abr, abi], axis=0).reshape(2, nb, 1, gb * N).swapaxes(0, 1)
    a2 = jnp.stack([a2r, a2i], axis=0).reshape(2, nb, 1, gb * N).swapaxes(0, 1)
    a_second = jnp.where(first, 0.0, a1)
    a_carry = jnp.where(first, a1, a2)
    return bblk, cblk, a_second, a_carry


def _gelu_tanh(x):
    c = math.sqrt(2.0 / math.pi)
    return x * (0.5 * (1.0 + jnp.tanh(c * (x + 0.044715 * (x * x * x)))))


def _ssm_body(x_ref, mod_ref, g_ref, d_ref, bblk_ref, cblk_ref, asec_ref, acar_ref, z_ref,
              hs_scr, hb_scr, bu_scr, y_scr, carry_scr, *, tt, nb):
    i = pl.program_id(0)
    B, _, D = x_ref.shape
    tm = tt * B
    sub = V7X_SUBLANES
    cin = SSM_GROUPS_PER_BLOCK * SSM_GROUP
    ns = SSM_BLOCK_STATES
    nslab = D // V7X_LANES
    slab = lambda k: slice(k * V7X_LANES, (k + 1) * V7X_LANES)

    @pl.when(i == 0)
    def _():
        carry_scr[...] = jnp.zeros(carry_scr.shape, F32)

    for b in range(B):
        h = _norm_mod(x_ref[b], g_ref[...], mod_ref[b, 3:4, :], mod_ref[b, 4:5, :])
        for k in range(nslab):
            hs_scr[k, pl.ds(b, tt, stride=B), :] = h[:, slab(k)]
    for k in range(nslab):
        hb_scr[:, slab(k)] = hs_scr[k].astype(BF16)

    first = lax.broadcasted_iota(jnp.int32, (sub, SSM_SCAN_LANES), 0) < sub // 2

    for gb in range(nb):
        bu_scr[...] = jnp.dot(hb_scr[:, gb * cin:(gb + 1) * cin], bblk_ref[gb],
                              preferred_element_type=F32)
        for part in range(ns // SSM_SCAN_LANES):
            lo = part * SSM_SCAN_LANES
            re_sl = slice(lo, lo + SSM_SCAN_LANES)
            im_sl = slice(ns + lo, ns + lo + SSM_SCAN_LANES)
            mr, mi = asec_ref[gb, 0, :, re_sl], asec_ref[gb, 1, :, re_sl]
            ar, ai = acar_ref[gb, 0, :, re_sl], acar_ref[gb, 1, :, re_sl]

            def step(r, carry):
                hr, hi = carry
                rows = pl.ds(pl.multiple_of(r * sub, sub), sub)
                vr, vi = bu_scr[rows, re_sl], bu_scr[rows, im_sl]
                vr2, vi2 = pltpu.roll(vr, sub // 2, 0), pltpu.roll(vi, sub // 2, 0)
                sr = vr + (vr2 * mr - vi2 * mi) + (hr * ar - hi * ai)
                si = vi + (vr2 * mi + vi2 * mr) + (hr * ai + hi * ar)
                bu_scr[rows, re_sl] = sr
                bu_scr[rows, im_sl] = si
                return (jnp.where(first, pltpu.roll(sr, sub // 2, 0), sr),
                        jnp.where(first, pltpu.roll(si, sub // 2, 0), si))

            hr, hi = lax.fori_loop(0, tm // sub, step,
                                   (carry_scr[gb, 0, :, re_sl], carry_scr[gb, 1, :, re_sl]),
                                   unroll=2)
            carry_scr[gb, 0, :, re_sl] = hr
            carry_scr[gb, 1, :, re_sl] = hi
        y_scr[:, gb * cin:(gb + 1) * cin] = jnp.dot(bu_scr[...].astype(BF16), cblk_ref[gb],
                                                    preferred_element_type=F32)

    for k in range(nslab):
        hs_scr[k] = _gelu_tanh(y_scr[:, slab(k)] + d_ref[:, slab(k)] * hs_scr[k])
    for b in range(B):
        for k in range(nslab):
            z_ref[b, :, slab(k)] = hs_scr[k, pl.ds(b, tt, stride=B), :].astype(BF16)


def _ssm(xv, mod, norm_g, d_skip, bblk, cblk, a_second, a_carry, *, layer, j, B, S, tt):
    D = norm_g.shape[-1]
    assert V7X_SUBLANES == 2 * B, "the scan packs 2 timesteps x B batches per 8-sublane row-group"
    nb = bblk.shape[0]
    tt = min(tt, S)
    tm = tt * B
    ns2 = 2 * SSM_BLOCK_STATES
    const = lambda *shape: _resident(shape, lambda i: (0,) * len(shape))
    vmem = 2 * tm * D * 4 + 2 * tm * D * 2 + 2 * tm * D * 4 + tm * D * 2 + 2 * tm * ns2 * 4 \
        + (bblk.size + cblk.size) * 2 + 2 * tm * D * 4 + (6 << 20)
    return pl.pallas_call(
        functools.partial(_ssm_body, tt=tt, nb=nb),
        out_shape=jax.ShapeDtypeStruct((B, S, D), BF16),
        grid=(S // tt,),
        in_specs=[
            pl.BlockSpec((B, tt, D), lambda i: (0, i, 0)),
            pl.BlockSpec((None, B, N_SUBLAYERS * 3, D), lambda i: (layer, 0, 0, 0)),
            pl.BlockSpec((None, None, 1, D), lambda i: (layer, 1, 0, 0)),
            pl.BlockSpec((None, 1, D), lambda i: (j, 0, 0)),
            const(*bblk.shape),
            const(*cblk.shape),
            const(*a_second.shape),
            const(*a_carry.shape),
        ],
        out_specs=pl.BlockSpec((B, tt, D), lambda i: (0, i, 0)),
        scratch_shapes=[
            pltpu.VMEM((D // V7X_LANES, tm, V7X_LANES), F32),
            pltpu.VMEM((tm, D), BF16),
            pltpu.VMEM((tm, ns2), F32),
            pltpu.VMEM((tm, D), F32),
            pltpu.VMEM((nb, 2, V7X_SUBLANES, SSM_BLOCK_STATES), F32),
        ],
        compiler_params=_compiler_params(1, vmem),
        name=f"ssm_l{layer}",
    )(xv.reshape(B, S, D), mod, norm_g.reshape(norm_g.shape[0], N_SUBLAYERS, 1, D),
      d_skip.reshape(-1, 1, D), bblk, cblk, a_second, a_carry)


def _glu_body(x_ref, z_ref, wv_ref, wg_ref, mod_ref, o_ref):
    z = z_ref[...]
    val = jnp.dot(z, wv_ref[...], preferred_element_type=F32)
    gt = jnp.dot(z, wg_ref[...], preferred_element_type=F32)
    o_ref[...] = x_ref[...] + mod_ref[5:6, :] * (val * jax.nn.sigmoid(gt))


def _glu(xv, zv, w, mod, *, layer, j, B, S, tm, tn):
    D = w.shape[1]
    tm = min(tm, S)
    nt, nn = S // tm, D // tn
    vmem = 4 * tm * tn * 4 + 2 * tm * D * 2 + 4 * D * tn * 2 + 3 * tm * tn * 4 + (6 << 20)
    return pl.pallas_call(
        _glu_body,
        out_shape=jax.ShapeDtypeStruct((B * S, D), F32),
        grid=(B, nt, nn),
        in_specs=[
            pl.BlockSpec((tm, tn), lambda b, ti, jn: (b * nt + ti, jn)),
            pl.BlockSpec((tm, D), lambda b, ti, jn: (b * nt + ti, 0)),
            pl.BlockSpec((None, D, tn), lambda b, ti, jn: (j, 0, jn)),
            pl.BlockSpec((None, D, tn), lambda b, ti, jn: (j, 0, nn + jn)),
            pl.BlockSpec((None, None, N_SUBLAYERS * 3, tn), lambda b, ti, jn: (layer, b, 0, jn)),
        ],
        out_specs=pl.BlockSpec((tm, tn), lambda b, ti, jn: (b * nt + ti, jn)),
        compiler_params=_compiler_params(3, vmem),
        name=f"glu_l{layer}",
    )(xv, zv, w, w, mod)


def kernel(x, c, positions, norm_g, ada_w, ada_b, ffn_w_in, ffn_w_out, attn_w_in, attn_w_out,
           attn_q_norm, attn_k_norm, attn_lambda, attn_subln, ssm_a_re, ssm_a_im, ssm_log_step,
           ssm_b_re, ssm_b_im, ssm_c_re, ssm_c_im, ssm_d, ssm_w_glu):
    B, S, D = x.shape
    depth = norm_g.shape[0]

    mod = _adaln(c, ada_w, ada_b)
    ffn_w_in = ffn_w_in.astype(BF16)
    ffn_w_out = ffn_w_out.astype(BF16)
    attn_w_in = attn_w_in.astype(BF16)
    attn_w_out = attn_w_out.astype(BF16)
    ssm_w_glu = ssm_w_glu.astype(BF16)
    pos = positions.reshape(B * S, 1)

    ffn = functools.partial(_ffn, mod=mod, norm_g=norm_g, w_in=ffn_w_in, w_out=ffn_w_out,
                            B=B, S=S, tm=1024, tf=512)
    xv = x.reshape(B * S, D)
    for i in range(depth):
        j = i // N_MIXERS
        xv = ffn(xv, layer=i, sl=0, which=0)
        if i % N_MIXERS == 0:
            lambda_init = 0.8 - 0.6 * math.exp(-0.3 * i)
            qkv = _qkv(xv, pos, mod, norm_g, attn_w_in, attn_q_norm, attn_k_norm,
                       layer=i, j=j, B=B, S=S, tm=512, tn=512)
            att = _attn(qkv, attn_lambda, attn_subln, j=j, B=B, S=S, D=D, tq=512,
                        lambda_init=lambda_init)
            xv = _oproj(xv, att, attn_w_out, mod, layer=i, j=j, B=B, S=S, tm=512)
        else:
            disc = _ssm_prep(ssm_a_re[j], ssm_a_im[j], ssm_log_step[j], ssm_b_re[j], ssm_b_im[j])
            operands = _ssm_block_operands(*disc, ssm_c_re[j], ssm_c_im[j])
            z = _ssm(xv, mod, norm_g, ssm_d, *operands, layer=i, j=j, B=B, S=S, tt=64)
            xv = _glu(xv, z.reshape(B * S, D), ssm_w_glu, mod, layer=i, j=j, B=B, S=S,
                      tm=512, tn=1024)
        xv = ffn(xv, layer=i, sl=2, which=1)
    return xv.reshape(B, S, D)
```

```python
import functools
import math

import numpy as np
import jax
import jax.numpy as jnp
from jax import lax
from jax.experimental import pallas as pl
from jax.experimental.pallas import tpu as pltpu

HEAD_DIM = 128
V_DIM = 2 * HEAD_DIM
ROT_DIM = HEAD_DIM // 4
ROPE_THETA = 500000.0
SSM_GROUP = 16
SSM_STATE = 64
N_SUBLAYERS = 3
N_MIXERS = 2
NORM_EPS = 1e-6
MACARON_WEIGHT = 0.5

V7X_LANES = 128
V7X_SUBLANES = 8
V7X_MXU_DIM = 256
V7X_VMEM_BYTES = 64 * 1024 * 1024

F32 = jnp.float32
BF16 = jnp.bfloat16
NEG_BIG = -1e30

SSM_GROUPS_PER_BLOCK = V7X_LANES // SSM_GROUP
SSM_BLOCK_STATES = SSM_GROUPS_PER_BLOCK * SSM_STATE
SSM_BLOCKS_PER_READOUT = V7X_MXU_DIM // V7X_LANES
SSM_STATE_BUFFERS = 2 * SSM_BLOCKS_PER_READOUT


def _compiler_params(n_axes, vmem_bytes):
    limit = min(int(vmem_bytes), V7X_VMEM_BYTES - 4 * 1024 * 1024)
    return pltpu.CompilerParams(dimension_semantics=("arbitrary",) * n_axes,
                                vmem_limit_bytes=limit)


def _resident(block_shape, index_map):
    return pl.BlockSpec(block_shape, index_map, pipeline_mode=pl.Buffered(1))


def _norm_mod(x, g, shift, scale):
    ms = jnp.mean(x * x, axis=-1, keepdims=True)
    y = x * lax.rsqrt(ms + NORM_EPS) * g
    return y * (1.0 + scale) + shift


def _adaln_body(c_ref, w_ref, b_ref, o_ref):
    c = c_ref[...]
    cond = (c * jax.nn.sigmoid(c)).astype(BF16)
    o_ref[...] = jnp.dot(cond, w_ref[...].astype(BF16), preferred_element_type=F32) + b_ref[...]


def _adaln(c, ada_w, ada_b):
    L, D, N = ada_w.shape
    B = c.shape[0]
    rows = V7X_SUBLANES * ((B + V7X_SUBLANES - 1) // V7X_SUBLANES)
    c_pad = jnp.pad(c, ((0, rows - B), (0, 0)))
    tn = D
    out = pl.pallas_call(
        _adaln_body,
        out_shape=jax.ShapeDtypeStruct((L, rows, N), F32),
        grid=(L, N // tn),
        in_specs=[
            pl.BlockSpec((rows, D), lambda l, j: (0, 0)),
            pl.BlockSpec((None, D, tn), lambda l, j: (l, 0, j)),
            pl.BlockSpec((None, 1, tn), lambda l, j: (l, 0, j)),
        ],
        out_specs=pl.BlockSpec((None, rows, tn), lambda l, j: (l, 0, j)),
        compiler_params=_compiler_params(2, 2 * D * tn * 4 + D * tn * 2 + (8 << 20)),
        name="adaln",
    )(c_pad, ada_w, ada_b.reshape(L, 1, N))
    return out[:, :B].reshape(L, B, N_SUBLAYERS * 3, D)


def _ffn_body(x_ref, mod_ref, g_ref, wa_ref, wb_ref, wo_ref, o_ref, h_scr, *, sl):
    j = pl.program_id(2)
    nj = pl.num_programs(2)

    @pl.when(j == 0)
    def _():
        h = _norm_mod(x_ref[...], g_ref[...], mod_ref[3 * sl:3 * sl + 1, :],
                      mod_ref[3 * sl + 1:3 * sl + 2, :])
        h_scr[...] = h.astype(BF16)
        o_ref[...] = jnp.zeros(o_ref.shape, F32)

    h = h_scr[...]
    a = jnp.dot(h, wa_ref[...], preferred_element_type=F32)
    b = jnp.dot(h, wb_ref[...], preferred_element_type=F32)
    act = (a * jax.nn.sigmoid(a) * b).astype(BF16)
    o_ref[...] += jnp.dot(act, wo_ref[...], preferred_element_type=F32)

    @pl.when(j == nj - 1)
    def _():
        gate = mod_ref[3 * sl + 2:3 * sl + 3, :]
        o_ref[...] = x_ref[...] + (MACARON_WEIGHT * gate) * o_ref[...]


def _ffn(xv, mod, norm_g, w_in, w_out, *, layer, sl, which, B, S, tm, tf):
    D = norm_g.shape[-1]
    F = w_out.shape[2]
    tm = min(tm, S)
    nt, nj = S // tm, F // tf
    rows = lambda b, ti, j: (b * nt + ti, 0)
    vmem = 3 * tm * D * 4 + tm * D * 2 + 2 * 3 * D * tf * 2 + 4 * tm * tf * 4 + (6 << 20)
    return pl.pallas_call(
        functools.partial(_ffn_body, sl=sl),
        out_shape=jax.ShapeDtypeStruct((B * S, D), F32),
        grid=(B, nt, nj),
        in_specs=[
            _resident((tm, D), rows),
            pl.BlockSpec((None, None, N_SUBLAYERS * 3, D), lambda b, ti, j: (layer, b, 0, 0)),
            pl.BlockSpec((None, None, 1, D), lambda b, ti, j: (layer, sl, 0, 0)),
            pl.BlockSpec((None, None, D, tf), lambda b, ti, j: (layer, which, 0, j)),
            pl.BlockSpec((None, None, D, tf), lambda b, ti, j: (layer, which, 0, nj + j)),
            pl.BlockSpec((None, None, tf, D), lambda b, ti, j: (layer, which, j, 0)),
        ],
        out_specs=pl.BlockSpec((tm, D), rows),
        scratch_shapes=[pltpu.VMEM((tm, D), BF16)],
        compiler_params=_compiler_params(3, vmem),
        name=f"ffn_l{layer}_s{sl}",
    )(xv, mod, norm_g.reshape(norm_g.shape[0], N_SUBLAYERS, 1, D), w_in, w_in, w_out)


def _qkv_body(x_ref, pos_ref, invf_ref, mod_ref, g_ref, w_ref, gain_ref, o_ref, *, tn, q_mult):
    D = x_ref.shape[-1]
    half = ROT_DIM // 2
    h = _norm_mod(x_ref[...], g_ref[...], mod_ref[3:4, :], mod_ref[4:5, :]).astype(BF16)

    ang = pos_ref[...].astype(F32) * invf_ref[...]
    c, s = jnp.cos(ang), jnp.sin(ang)
    lane = lax.broadcasted_iota(jnp.int32, ang.shape, 1)
    cos = jnp.where(lane < ROT_DIM, c, 1.0)
    sin_a = jnp.where(lane < half, -s, 0.0)
    sin_b = jnp.where((lane >= half) & (lane < ROT_DIM), s, 0.0)
    gains = (gain_ref[0] * q_mult, gain_ref[1])

    for jn in range(w_ref.shape[-1] // tn):
        cols = slice(jn * tn, (jn + 1) * tn)
        acc = jnp.dot(h, w_ref[:, cols], preferred_element_type=F32)
        which = (jn * tn) // D
        if which == 2:
            o_ref[:, cols] = acc.astype(BF16)
            continue
        for ch in range(tn // HEAD_DIM):
            t = acc[:, ch * HEAD_DIM:(ch + 1) * HEAD_DIM]
            ms = jnp.mean(t * t, axis=-1, keepdims=True)
            y = t * lax.rsqrt(ms + NORM_EPS) * gains[which]
            r = (y * cos + pltpu.roll(y, HEAD_DIM - half, 1) * sin_a
                 + pltpu.roll(y, half, 1) * sin_b)
            lo = jn * tn + ch * HEAD_DIM
            o_ref[:, lo:lo + HEAD_DIM] = r.astype(BF16)


def _qkv(xv, pos, mod, norm_g, w, q_gain, k_gain, *, layer, j, B, S, tm, tn):
    D = norm_g.shape[-1]
    N = w.shape[-1]
    tm = min(tm, S)
    nt = S // tm
    half = ROT_DIM // 2
    inv_freq = ROPE_THETA ** (-np.arange(0, ROT_DIM, 2, dtype=np.float64) / ROT_DIM)
    invf = jnp.asarray(np.tile(inv_freq, HEAD_DIM // half)[None, :], dtype=F32)
    gains = jnp.stack([q_gain, k_gain], axis=1).reshape(-1, 2, 1, HEAD_DIM)
    q_mult = HEAD_DIM ** -0.5 * math.log2(math.e)
    rows = lambda b, ti: (b * nt + ti, 0)
    vmem = 2 * tm * D * 4 + tm * D * 2 + D * N * 2 + 2 * tm * N * 2 + 4 * tm * tn * 4 \
        + 8 * tm * V7X_LANES * 4 + (6 << 20)
    return pl.pallas_call(
        functools.partial(_qkv_body, tn=tn, q_mult=q_mult),
        out_shape=jax.ShapeDtypeStruct((B * S, N), BF16),
        grid=(B, nt),
        in_specs=[
            pl.BlockSpec((tm, D), rows),
            pl.BlockSpec((tm, 1), rows),
            pl.BlockSpec((1, HEAD_DIM), lambda b, ti: (0, 0)),
            pl.BlockSpec((None, None, N_SUBLAYERS * 3, D), lambda b, ti: (layer, b, 0, 0)),
            pl.BlockSpec((None, None, 1, D), lambda b, ti: (layer, 1, 0, 0)),
            _resident((None, D, N), lambda b, ti: (j, 0, 0)),
            pl.BlockSpec((None, 2, 1, HEAD_DIM), lambda b, ti: (j, 0, 0, 0)),
        ],
        out_specs=pl.BlockSpec((tm, N), rows),
        compiler_params=_compiler_params(2, vmem),
        name=f"qkv_l{layer}",
    )(xv, pos, invf, mod, norm_g.reshape(norm_g.shape[0], N_SUBLAYERS, 1, D), w, gains)


def _attn_body(q_ref, k_ref, v_ref, lam_ref, sub_ref, o_ref, m_scr, l_scr, acc_scr,
               sa_scr, sb_scr, *, tq, lambda_init):
    qi = pl.program_id(2)
    m_scr[...] = jnp.full(m_scr.shape, NEG_BIG, F32)
    l_scr[...] = jnp.zeros(l_scr.shape, F32)
    acc_scr[...] = jnp.zeros(acc_scr.shape, F32)
    q = q_ref[...]
    key = lax.broadcasted_iota(jnp.int32, (tq, tq), 0)
    qry = lax.broadcasted_iota(jnp.int32, (tq, tq), 1)
    causal = key <= qry

    def scores(kb, s_scr):
        kblk = k_ref[pl.ds(pl.multiple_of(kb * tq, tq), tq), :]
        for c in range(2):
            qc = q[:, c * HEAD_DIM:(c + 1) * HEAD_DIM]
            kc = kblk[:, c * HEAD_DIM:(c + 1) * HEAD_DIM]
            s_scr[c] = lax.dot_general(kc, qc, (((1,), (1,)), ((), ())),
                                       preferred_element_type=F32)

    def consume(kb, s_scr, masked):
        vblk = v_ref[pl.ds(pl.multiple_of(kb * tq, tq), tq), :]
        for c in range(2):
            s = s_scr[c]
            if masked:
                s = jnp.where(causal, s, NEG_BIG)
            m_prev = m_scr[c]
            m_new = jnp.maximum(m_prev, jnp.max(s, axis=0, keepdims=True))
            p = jnp.exp2(s - m_new)
            alpha = jnp.exp2(m_prev - m_new)
            l_scr[c] = alpha * l_scr[c] + jnp.sum(p, axis=0, keepdims=True)
            pv = lax.dot_general(vblk, p.astype(BF16), (((0,), (0,)), ((), ())),
                                 preferred_element_type=F32)
            acc_scr[c] = alpha * acc_scr[c] + pv
            m_scr[c] = m_new

    scores(0, sa_scr)

    def pair(p, carry):
        scores(2 * p + 1, sb_scr)
        consume(2 * p, sa_scr, False)
        scores(2 * p + 2, sa_scr)
        consume(2 * p + 1, sb_scr, False)
        return carry

    lax.fori_loop(0, qi // 2, pair, 0)

    @pl.when(qi % 2 == 1)
    def _():
        scores(qi, sb_scr)
        consume(qi - 1, sa_scr, False)
        consume(qi, sb_scr, True)

    @pl.when(qi % 2 == 0)
    def _():
        consume(qi, sa_scr, True)

    lv = lam_ref[...]
    lam = (jnp.exp(jnp.sum(lv[0:1] * lv[1:2], axis=-1, keepdims=True))
           - jnp.exp(jnp.sum(lv[2:3] * lv[3:4], axis=-1, keepdims=True)) + lambda_init)
    o = acc_scr[0] / l_scr[0] - lam * (acc_scr[1] / l_scr[1])
    ms = jnp.mean(o * o, axis=0, keepdims=True)
    o = (o * lax.rsqrt(ms + NORM_EPS)).T
    o_ref[...] = ((o * sub_ref[...]) * (1.0 - lambda_init)).astype(BF16)


def _attn(qkv, lam, subln, *, j, B, S, D, tq, lambda_init):
    H = D // V_DIM
    tq = min(tq, S)
    nq = S // tq
    vmem = 2 * 2 * S * V_DIM * 2 + 4 * tq * V_DIM * 2 + 2 * tq * V_DIM * 4 \
        + 4 * tq * V7X_SUBLANES * 4 + 12 * tq * tq * 4 + (6 << 20)
    return pl.pallas_call(
        functools.partial(_attn_body, tq=tq, lambda_init=lambda_init),
        out_shape=jax.ShapeDtypeStruct((B * S, D), BF16),
        grid=(B, H, nq),
        in_specs=[
            pl.BlockSpec((tq, V_DIM), lambda b, h, qi: (b * nq + qi, h)),
            pl.BlockSpec((S, V_DIM), lambda b, h, qi: (b, H + h)),
            pl.BlockSpec((S, V_DIM), lambda b, h, qi: (b, 2 * H + h)),
            pl.BlockSpec((None, 4, HEAD_DIM), lambda b, h, qi: (j, 0, 0)),
            pl.BlockSpec((None, 1, V_DIM), lambda b, h, qi: (j, 0, 0)),
        ],
        out_specs=pl.BlockSpec((tq, V_DIM), lambda b, h, qi: (b * nq + qi, h)),
        scratch_shapes=[pltpu.VMEM((2, 1, tq), F32), pltpu.VMEM((2, 1, tq), F32),
                        pltpu.VMEM((2, V_DIM, tq), F32),
                        pltpu.VMEM((2, tq, tq), F32), pltpu.VMEM((2, tq, tq), F32)],
        compiler_params=_compiler_params(3, vmem),
        name="diff_attn",
    )(qkv, qkv, qkv, lam, subln.reshape(-1, 1, V_DIM))


def _oproj_body(x_ref, a_ref, w_ref, mod_ref, o_ref):
    m = jnp.dot(a_ref[...], w_ref[...], preferred_element_type=F32)
    o_ref[...] = x_ref[...] + mod_ref[5:6, :] * m


def _oproj(xv, a, w, mod, *, layer, j, B, S, tm):
    D = xv.shape[-1]
    tm = min(tm, S)
    nt = S // tm
    rows = lambda b, ti: (b * nt + ti, 0)
    vmem = 4 * tm * D * 4 + 2 * tm * D * 2 + D * D * 2 + tm * D * 4 + (6 << 20)
    return pl.pallas_call(
        _oproj_body,
        out_shape=jax.ShapeDtypeStruct((B * S, D), F32),
        grid=(B, nt),
        in_specs=[
            pl.BlockSpec((tm, D), rows),
            pl.BlockSpec((tm, D), rows),
            _resident((None, D, D), lambda b, ti: (j, 0, 0)),
            pl.BlockSpec((None, None, N_SUBLAYERS * 3, D), lambda b, ti: (layer, b, 0, 0)),
        ],
        out_specs=pl.BlockSpec((tm, D), rows),
        compiler_params=_compiler_params(2, vmem),
        name=f"attn_out_l{layer}",
    )(xv, a, w, mod)


def _ssm_prep_body(ar_ref, ai_ref, ls_ref, arr_ref, air_ref, lsr_ref, br_ref, bi_ref,
                   abr_ref, abi_ref, a2r_ref, a2i_ref, bbr_ref, bbi_ref, abbr_ref, abbi_ref):
    def abar(ar, ai, ls):
        dt = jnp.exp(ls)
        mag = jnp.exp(dt * ar)
        return mag * jnp.cos(dt * ai), mag * jnp.sin(dt * ai)

    re, im = abar(ar_ref[...], ai_ref[...], ls_ref[...])
    abr_ref[...] = re
    abi_ref[...] = im
    a2r_ref[...] = re * re - im * im
    a2i_ref[...] = 2.0 * (re * im)

    ar, ai = arr_ref[...], air_ref[...]
    re, im = abar(ar, ai, lsr_ref[...])
    den = ar * ar + ai * ai
    num_re = re - 1.0
    coef_re = (num_re * ar + im * ai) / den
    coef_im = (im * ar - num_re * ai) / den
    br, bi = br_ref[...], bi_ref[...]
    bbr = coef_re * br - coef_im * bi
    bbi = coef_re * bi + coef_im * br
    bbr_ref[...] = bbr
    bbi_ref[...] = bbi
    abbr_ref[...] = re * bbr - im * bbi
    abbi_ref[...] = re * bbi + im * bbr


def _ssm_prep(a_re, a_im, log_step, b_re, b_im):
    G, N = a_re.shape
    P = b_re.shape[-1]
    rep = lambda v: jnp.repeat(v, P, axis=0)
    flat = lambda v: jnp.swapaxes(v, 1, 2).reshape(G * P, N)
    ls = log_step.reshape(G, 1)
    small = jax.ShapeDtypeStruct((G, N), F32)
    big = jax.ShapeDtypeStruct((G * P, N), F32)
    return pl.pallas_call(
        _ssm_prep_body,
        out_shape=(small, small, small, small, big, big, big, big),
        name="ssm_discretise",
    )(a_re, a_im, ls, rep(a_re), rep(a_im), rep(ls), flat(b_re), flat(b_im))


def _ssm_block_operands(abr, abi, a2r, a2i, bbr, bbi, abbr, abbi, c_re, c_im):
    G, N = abr.shape
    P = SSM_GROUP
    gb = SSM_GROUPS_PER_BLOCK
    nb = G // gb
    eye = jnp.eye(gb, dtype=F32)

    def b_block(v):
        v = v.reshape(nb, gb, P, 1, N) * eye[None, :, None, :, None]
        return v.reshape(nb, gb * P, gb * N)

    def c_block(v):
        v = jnp.swapaxes(v.reshape(nb, gb, P, N), 2, 3)
        v = v.reshape(nb, gb, N, 1, P) * eye[None, :, None, :, None]
        return v.reshape(nb, gb * N, gb * P)

    bcur = jnp.concatenate([b_block(bbr), b_block(bbi)], axis=-1)
    bprev = jnp.concatenate([b_block(abbr), b_block(abbi)], axis=-1)
    bcat = jnp.concatenate([bcur, bprev], axis=1).astype(BF16)

    c8 = jnp.concatenate([c_block(c_re), -c_block(c_im)], axis=1)
    npair = SSM_BLOCKS_PER_READOUT
    eye2 = jnp.eye(npair, dtype=F32)
    cpair = c8.reshape(nb // npair, npair, 2 * gb * N, 1, gb * P) * eye2[None, :, None, :, None]
    cpair = cpair.reshape(nb // npair, npair * 2 * gb * N, npair * gb * P).astype(BF16)

    first = (lax.broadcasted_iota(jnp.int32, (1, 1, V7X_SUBLANES, 1), 2) < V7X_SUBLANES // 2)
    a1 = jnp.stack([abr, abi], axis=0).reshape(2, nb, 1, gb * N).swapaxes(0, 1)
    a2 = jnp.stack([a2r, a2i], axis=0).reshape(2, nb, 1, gb * N).swapaxes(0, 1)
    a_carry = jnp.where(first, a1, a2)
    return bcat, cpair, a_carry


def _gelu_tanh(x):
    c = math.sqrt(2.0 / math.pi)
    return x * (0.5 * (1.0 + jnp.tanh(c * (x + 0.044715 * (x * x * x)))))


def _ssm_body(x_ref, mod_ref, g_ref, d_ref, bcat_ref, cpair_ref, acar_ref, z_ref,
              hs_scr, hcat_scr, st_scr, y_scr, carry_scr, *, tt, nb):
    i = pl.program_id(0)
    B, _, D = x_ref.shape
    tm = tt * B
    sub = V7X_SUBLANES
    ns = SSM_BLOCK_STATES
    kin = 2 * V7X_LANES
    npair = SSM_BLOCKS_PER_READOUT
    cout = npair * V7X_LANES
    slab = lambda k: slice(k * V7X_LANES, (k + 1) * V7X_LANES)

    @pl.when(i == 0)
    def _():
        carry_scr[...] = jnp.zeros(carry_scr.shape, F32)

    for b in range(B):
        h = _norm_mod(x_ref[b], g_ref[...], mod_ref[b, 3:4, :], mod_ref[b, 4:5, :])
        for k in range(nb):
            hs_scr[k, pl.ds(b, tt, stride=B), :] = h[:, slab(k)]
    second = (lax.broadcasted_iota(jnp.int32, (tm, V7X_LANES), 0) & (sub // 2)) != 0
    for k in range(nb):
        hk = hs_scr[k]
        hcat_scr[:, k * kin:k * kin + V7X_LANES] = hk.astype(BF16)
        hcat_scr[:, k * kin + V7X_LANES:(k + 1) * kin] = jnp.where(
            second, pltpu.roll(hk, sub // 2, 0), 0.0).astype(BF16)

    first = lax.broadcasted_iota(jnp.int32, (sub, ns), 0) < sub // 2
    re_sl, im_sl = slice(0, ns), slice(ns, 2 * ns)

    for sb in range(nb):
        buf = sb % SSM_STATE_BUFFERS
        st_scr[buf] = jnp.dot(hcat_scr[:, sb * kin:(sb + 1) * kin], bcat_ref[sb],
                              preferred_element_type=F32)
        ar, ai = acar_ref[sb, 0], acar_ref[sb, 1]

        def step(r, carry, buf=buf, ar=ar, ai=ai):
            hr, hi = carry
            rows = pl.ds(pl.multiple_of(r * sub, sub), sub)
            sr = st_scr[buf, rows, re_sl] + (hr * ar - hi * ai)
            si = st_scr[buf, rows, im_sl] + (hr * ai + hi * ar)
            st_scr[buf, rows, re_sl] = sr
            st_scr[buf, rows, im_sl] = si
            return (jnp.where(first, pltpu.roll(sr, sub // 2, 0), sr),
                    jnp.where(first, pltpu.roll(si, sub // 2, 0), si))

        hr, hi = lax.fori_loop(0, tm // sub, step, (carry_scr[sb, 0], carry_scr[sb, 1]),
                               unroll=True)
        carry_scr[sb, 0] = hr
        carry_scr[sb, 1] = hi

        if sb % npair == npair - 1:
            pair = sb // npair
            acc = None
            for e in range(npair):
                states = st_scr[(sb - npair + 1 + e) % SSM_STATE_BUFFERS].astype(BF16)
                part = jnp.dot(states, cpair_ref[pair, e * 2 * ns:(e + 1) * 2 * ns, :],
                               preferred_element_type=F32)
                acc = part if acc is None else acc + part
            y_scr[:, pair * cout:(pair + 1) * cout] = acc

    for k in range(nb):
        hs_scr[k] = _gelu_tanh(y_scr[:, slab(k)] + d_ref[:, slab(k)] * hs_scr[k])
    for b in range(B):
        for k in range(nb):
            z_ref[b, :, slab(k)] = hs_scr[k, pl.ds(b, tt, stride=B), :].astype(BF16)


def _ssm(xv, mod, norm_g, d_skip, bcat, cpair, a_carry, *, layer, j, B, S, tt):
    D = norm_g.shape[-1]
    assert V7X_SUBLANES == 2 * B, "the scan packs 2 timesteps x B batches per 8-sublane row-group"
    nb = bcat.shape[0]
    assert nb == D // V7X_LANES
    tt = min(tt, S)
    tm = tt * B
    ns2 = 2 * SSM_BLOCK_STATES
    const = lambda *shape: _resident(shape, lambda i: (0,) * len(shape))
    vmem = 2 * tm * D * 4 + 2 * tm * D * 2 + 2 * tm * D * 4 + 2 * tm * D * 2 \
        + (SSM_STATE_BUFFERS + 2) * tm * ns2 * 4 + (bcat.size + cpair.size) * 2 + 2 * tm * D * 4 \
        + (6 << 20)
    return pl.pallas_call(
        functools.partial(_ssm_body, tt=tt, nb=nb),
        out_shape=jax.ShapeDtypeStruct((B, S, D), BF16),
        grid=(S // tt,),
        in_specs=[
            pl.BlockSpec((B, tt, D), lambda i: (0, i, 0)),
            pl.BlockSpec((None, B, N_SUBLAYERS * 3, D), lambda i: (layer, 0, 0, 0)),
            pl.BlockSpec((None, None, 1, D), lambda i: (layer, 1, 0, 0)),
            pl.BlockSpec((None, 1, D), lambda i: (j, 0, 0)),
            const(*bcat.shape),
            const(*cpair.shape),
            const(*a_carry.shape),
        ],
        out_specs=pl.BlockSpec((B, tt, D), lambda i: (0, i, 0)),
        scratch_shapes=[
            pltpu.VMEM((nb, tm, V7X_LANES), F32),
            pltpu.VMEM((tm, 2 * D), BF16),
            pltpu.VMEM((SSM_STATE_BUFFERS, tm, ns2), F32),
            pltpu.VMEM((tm, D), F32),
            pltpu.VMEM((nb, 2, V7X_SUBLANES, SSM_BLOCK_STATES), F32),
        ],
        compiler_params=_compiler_params(1, vmem),
        name=f"ssm_l{layer}",
    )(xv.reshape(B, S, D), mod, norm_g.reshape(norm_g.shape[0], N_SUBLAYERS, 1, D),
      d_skip.reshape(-1, 1, D), bcat, cpair, a_carry)


def _glu_body(x_ref, z_ref, wv_ref, wg_ref, mod_ref, o_ref):
    z = z_ref[...]
    val = jnp.dot(z, wv_ref[...], preferred_element_type=F32)
    gt = jnp.dot(z, wg_ref[...], preferred_element_type=F32)
    o_ref[...] = x_ref[...] + mod_ref[5:6, :] * (val * jax.nn.sigmoid(gt))


def _glu(xv, zv, w, mod, *, layer, j, B, S, tm, tn):
    D = w.shape[1]
    tm = min(tm, S)
    nt, nn = S // tm, D // tn
    vmem = 4 * tm * tn * 4 + 2 * tm * D * 2 + 4 * D * tn * 2 + 3 * tm * tn * 4 + (6 << 20)
    return pl.pallas_call(
        _glu_body,
        out_shape=jax.ShapeDtypeStruct((B * S, D), F32),
        grid=(B, nt, nn),
        in_specs=[
            pl.BlockSpec((tm, tn), lambda b, ti, jn: (b * nt + ti, jn)),
            pl.BlockSpec((tm, D), lambda b, ti, jn: (b * nt + ti, 0)),
            pl.BlockSpec((None, D, tn), lambda b, ti, jn: (j, 0, jn)),
            pl.BlockSpec((None, D, tn), lambda b, ti, jn: (j, 0, nn + jn)),
            pl.BlockSpec((None, None, N_SUBLAYERS * 3, tn), lambda b, ti, jn: (layer, b, 0, jn)),
        ],
        out_specs=pl.BlockSpec((tm, tn), lambda b, ti, jn: (b * nt + ti, jn)),
        compiler_params=_compiler_params(3, vmem),
        name=f"glu_l{layer}",
    )(xv, zv, w, w, mod)


def kernel(x, c, positions, norm_g, ada_w, ada_b, ffn_w_in, ffn_w_out, attn_w_in, attn_w_out,
           attn_q_norm, attn_k_norm, attn_lambda, attn_subln, ssm_a_re, ssm_a_im, ssm_log_step,
           ssm_b_re, ssm_b_im, ssm_c_re, ssm_c_im, ssm_d, ssm_w_glu):
    B, S, D = x.shape
    depth = norm_g.shape[0]

    mod = _adaln(c, ada_w, ada_b)
    ffn_w_in = ffn_w_in.astype(BF16)
    ffn_w_out = ffn_w_out.astype(BF16)
    attn_w_in = attn_w_in.astype(BF16)
    attn_w_out = attn_w_out.astype(BF16)
    ssm_w_glu = ssm_w_glu.astype(BF16)
    pos = positions.reshape(B * S, 1)

    ffn = functools.partial(_ffn, mod=mod, norm_g=norm_g, w_in=ffn_w_in, w_out=ffn_w_out,
                            B=B, S=S, tm=1024, tf=512)
    xv = x.reshape(B * S, D)
    for i in range(depth):
        j = i // N_MIXERS
        xv = ffn(xv, layer=i, sl=0, which=0)
        if i % N_MIXERS == 0:
            lambda_init = 0.8 - 0.6 * math.exp(-0.3 * i)
            qkv = _qkv(xv, pos, mod, norm_g, attn_w_in, attn_q_norm, attn_k_norm,
                       layer=i, j=j, B=B, S=S, tm=512, tn=512)
            att = _attn(qkv, attn_lambda, attn_subln, j=j, B=B, S=S, D=D, tq=512,
                        lambda_init=lambda_init)
            xv = _oproj(xv, att, attn_w_out, mod, layer=i, j=j, B=B, S=S, tm=512)
        else:
            disc = _ssm_prep(ssm_a_re[j], ssm_a_im[j], ssm_log_step[j], ssm_b_re[j], ssm_b_im[j])
            operands = _ssm_block_operands(*disc, ssm_c_re[j], ssm_c_im[j])
            z = _ssm(xv, mod, norm_g, ssm_d, *operands, layer=i, j=j, B=B, S=S, tt=64)
            xv = _glu(xv, z.reshape(B * S, D), ssm_w_glu, mod, layer=i, j=j, B=B, S=S,
                      tm=512, tn=1024)
        xv = ffn(xv, layer=i, sl=2, which=1)
    return xv.reshape(B, S, D)
```

```python
import functools
import math

import numpy as np
import jax
import jax.numpy as jnp
from jax import lax
from jax.experimental import pallas as pl
from jax.experimental.pallas import tpu as pltpu

HEAD_DIM = 128
V_DIM = 2 * HEAD_DIM
ROT_DIM = HEAD_DIM // 4
ROPE_THETA = 500000.0
SSM_GROUP = 16
SSM_STATE = 64
N_SUBLAYERS = 3
N_MIXERS = 2
NORM_EPS = 1e-6
MACARON_WEIGHT = 0.5

V7X_LANES = 128
V7X_SUBLANES = 8
V7X_MXU_DIM = 256
V7X_VMEM_BYTES = 64 * 1024 * 1024
V7X_SCOPED_VMEM_CAP = 56 * 1024 * 1024

F32 = jnp.float32
BF16 = jnp.bfloat16
NEG_BIG = -1e30

SSM_GROUPS_PER_BLOCK = V7X_LANES // SSM_GROUP
SSM_BLOCK_STATES = SSM_GROUPS_PER_BLOCK * SSM_STATE
SSM_BLOCKS_PER_READOUT = V7X_MXU_DIM // V7X_LANES
SSM_STATE_BUFFERS = 2 * SSM_BLOCKS_PER_READOUT

FFN_NORM_ROWS = 32


def _compiler_params(n_axes, vmem_bytes, flags=None):
    limit = min(int(vmem_bytes), V7X_SCOPED_VMEM_CAP)
    return pltpu.CompilerParams(dimension_semantics=("arbitrary",) * n_axes,
                                vmem_limit_bytes=limit, flags=flags)


def _resident(block_shape, index_map):
    return pl.BlockSpec(block_shape, index_map, pipeline_mode=pl.Buffered(1))


def _norm_mod(x, g, shift, scale):
    ms = jnp.mean(x * x, axis=-1, keepdims=True)
    y = x * lax.rsqrt(ms + NORM_EPS) * g
    return y * (1.0 + scale) + shift


def _adaln_body(c_ref, w_ref, b_ref, o_ref):
    c = c_ref[...]
    cond = (c * jax.nn.sigmoid(c)).astype(BF16)
    o_ref[...] = jnp.dot(cond, w_ref[...].astype(BF16), preferred_element_type=F32) + b_ref[...]


def _adaln(c, ada_w, ada_b):
    L, D, N = ada_w.shape
    B = c.shape[0]
    rows = V7X_SUBLANES * ((B + V7X_SUBLANES - 1) // V7X_SUBLANES)
    c_pad = jnp.pad(c, ((0, rows - B), (0, 0)))
    tn = D
    out = pl.pallas_call(
        _adaln_body,
        out_shape=jax.ShapeDtypeStruct((L, rows, N), F32),
        grid=(L, N // tn),
        in_specs=[
            pl.BlockSpec((rows, D), lambda l, j: (0, 0)),
            pl.BlockSpec((None, D, tn), lambda l, j: (l, 0, j)),
            pl.BlockSpec((None, 1, tn), lambda l, j: (l, 0, j)),
        ],
        out_specs=pl.BlockSpec((None, rows, tn), lambda l, j: (l, 0, j)),
        compiler_params=_compiler_params(2, 2 * D * tn * 4 + D * tn * 2 + (8 << 20)),
        name="adaln",
    )(c_pad, ada_w, ada_b.reshape(L, 1, N))
    return out[:, :B].reshape(L, B, N_SUBLAYERS * 3, D)


def _ffn_body(x_ref, mod_ref, g_ref, wa_ref, wb_ref, wo_ref, o_ref, h_scr, *, sl):
    j = pl.program_id(2)
    nj = pl.num_programs(2)

    @pl.when(j == 0)
    def _():
        def chunk(r, carry):
            rows = pl.ds(pl.multiple_of(r * FFN_NORM_ROWS, FFN_NORM_ROWS), FFN_NORM_ROWS)
            h = _norm_mod(x_ref[rows, :], g_ref[...], mod_ref[3 * sl:3 * sl + 1, :],
                          mod_ref[3 * sl + 1:3 * sl + 2, :])
            h_scr[rows, :] = h.astype(BF16)
            o_ref[rows, :] = jnp.zeros((FFN_NORM_ROWS, o_ref.shape[-1]), F32)
            return carry

        lax.fori_loop(0, x_ref.shape[0] // FFN_NORM_ROWS, chunk, 0, unroll=2)

    h = h_scr[...]
    a = jnp.dot(h, wa_ref[...], preferred_element_type=F32)
    b = jnp.dot(h, wb_ref[...], preferred_element_type=F32)
    act = (a * jax.nn.sigmoid(a) * b).astype(BF16)
    o_ref[...] += jnp.dot(act, wo_ref[...], preferred_element_type=F32)

    @pl.when(j == nj - 1)
    def _():
        gate = mod_ref[3 * sl + 2:3 * sl + 3, :]
        o_ref[...] = x_ref[...] + (MACARON_WEIGHT * gate) * o_ref[...]


def _column_tiled(w, tn):
    *lead, K, N = w.shape
    w = w.reshape(*lead, K, N // tn, tn)
    return jnp.swapaxes(w, -3, -2)


def _ffn(xv, mod, norm_g, w_in, w_out, *, layer, sl, which, B, S, tm):
    D = norm_g.shape[-1]
    F = w_out.shape[2]
    tf = w_in.shape[-1]
    tm = min(tm, S)
    nt, nj = S // tm, F // tf
    rows = lambda b, ti, j: (b * nt + ti, 0)
    vmem = 4 * tm * D * 4 + tm * D * 2 + 2 * 3 * D * tf * 2 + 4 * tm * tf * 4 + (4 << 20)
    return pl.pallas_call(
        functools.partial(_ffn_body, sl=sl),
        out_shape=jax.ShapeDtypeStruct((B * S, D), F32),
        grid=(B, nt, nj),
        in_specs=[
            pl.BlockSpec((tm, D), rows),
            pl.BlockSpec((None, None, N_SUBLAYERS * 3, D), lambda b, ti, j: (layer, b, 0, 0)),
            pl.BlockSpec((None, None, 1, D), lambda b, ti, j: (layer, sl, 0, 0)),
            pl.BlockSpec((None, None, None, D, tf), lambda b, ti, j: (layer, which, j, 0, 0)),
            pl.BlockSpec((None, None, None, D, tf), lambda b, ti, j: (layer, which, nj + j, 0, 0)),
            pl.BlockSpec((None, None, tf, D), lambda b, ti, j: (layer, which, j, 0)),
        ],
        out_specs=pl.BlockSpec((tm, D), rows),
        scratch_shapes=[pltpu.VMEM((tm, D), BF16)],
        compiler_params=_compiler_params(3, vmem),
        name=f"ffn_l{layer}_s{sl}",
    )(xv, mod, norm_g.reshape(norm_g.shape[0], N_SUBLAYERS, 1, D), w_in, w_in, w_out)


def _qkv_body(x_ref, pos_ref, invf_ref, mod_ref, g_ref, w_ref, gain_ref, o_ref, *, tn, q_mult):
    D = x_ref.shape[-1]
    half = ROT_DIM // 2
    h = _norm_mod(x_ref[...], g_ref[...], mod_ref[3:4, :], mod_ref[4:5, :]).astype(BF16)

    ang = pos_ref[...].astype(F32) * invf_ref[...]
    c, s = jnp.cos(ang), jnp.sin(ang)
    lane = lax.broadcasted_iota(jnp.int32, ang.shape, 1)
    cos = jnp.where(lane < ROT_DIM, c, 1.0)
    sin_a = jnp.where(lane < half, -s, 0.0)
    sin_b = jnp.where((lane >= half) & (lane < ROT_DIM), s, 0.0)
    gains = (gain_ref[0] * q_mult, gain_ref[1])

    for jn in range(w_ref.shape[-1] // tn):
        cols = slice(jn * tn, (jn + 1) * tn)
        acc = jnp.dot(h, w_ref[:, cols], preferred_element_type=F32)
        which = (jn * tn) // D
        if which == 2:
            o_ref[:, cols] = acc.astype(BF16)
            continue
        for ch in range(tn // HEAD_DIM):
            t = acc[:, ch * HEAD_DIM:(ch + 1) * HEAD_DIM]
            ms = jnp.mean(t * t, axis=-1, keepdims=True)
            y = t * lax.rsqrt(ms + NORM_EPS) * gains[which]
            r = (y * cos + pltpu.roll(y, HEAD_DIM - half, 1) * sin_a
                 + pltpu.roll(y, half, 1) * sin_b)
            lo = jn * tn + ch * HEAD_DIM
            o_ref[:, lo:lo + HEAD_DIM] = r.astype(BF16)


def _qkv(xv, pos, mod, norm_g, w, q_gain, k_gain, *, layer, j, B, S, tm, tn):
    D = norm_g.shape[-1]
    N = w.shape[-1]
    tm = min(tm, S)
    nt = S // tm
    half = ROT_DIM // 2
    inv_freq = ROPE_THETA ** (-np.arange(0, ROT_DIM, 2, dtype=np.float64) / ROT_DIM)
    invf = jnp.asarray(np.tile(inv_freq, HEAD_DIM // half)[None, :], dtype=F32)
    gains = jnp.stack([q_gain, k_gain], axis=1).reshape(-1, 2, 1, HEAD_DIM)
    q_mult = HEAD_DIM ** -0.5 * math.log2(math.e)
    rows = lambda b, ti: (b * nt + ti, 0)
    vmem = 2 * tm * D * 4 + tm * D * 2 + D * N * 2 + 2 * tm * N * 2 + 4 * tm * tn * 4 \
        + 8 * tm * V7X_LANES * 4 + (6 << 20)
    return pl.pallas_call(
        functools.partial(_qkv_body, tn=tn, q_mult=q_mult),
        out_shape=jax.ShapeDtypeStruct((B * S, N), BF16),
        grid=(B, nt),
        in_specs=[
            pl.BlockSpec((tm, D), rows),
            pl.BlockSpec((tm, 1), rows),
            pl.BlockSpec((1, HEAD_DIM), lambda b, ti: (0, 0)),
            pl.BlockSpec((None, None, N_SUBLAYERS * 3, D), lambda b, ti: (layer, b, 0, 0)),
            pl.BlockSpec((None, None, 1, D), lambda b, ti: (layer, 1, 0, 0)),
            _resident((None, D, N), lambda b, ti: (j, 0, 0)),
            pl.BlockSpec((None, 2, 1, HEAD_DIM), lambda b, ti: (j, 0, 0, 0)),
        ],
        out_specs=pl.BlockSpec((tm, N), rows),
        compiler_params=_compiler_params(2, vmem),
        name=f"qkv_l{layer}",
    )(xv, pos, invf, mod, norm_g.reshape(norm_g.shape[0], N_SUBLAYERS, 1, D), w, gains)


def _attn_body(q_ref, k_ref, v_ref, lam_ref, sub_ref, o_ref, m_scr, l_scr, acc_scr,
               sa_scr, sb_scr, *, tq, lambda_init):
    qi = pl.program_id(2)
    tk = tq // 2
    m_scr[...] = jnp.full(m_scr.shape, NEG_BIG, F32)
    l_scr[...] = jnp.zeros(l_scr.shape, F32)
    acc_scr[...] = jnp.zeros(acc_scr.shape, F32)
    q = q_ref[...]
    key = lax.broadcasted_iota(jnp.int32, (tk, tq), 0)
    qry = lax.broadcasted_iota(jnp.int32, (tk, tq), 1)
    causal = key <= qry

    def scores(kb, s_scr, lo=0):
        kblk = k_ref[pl.ds(pl.multiple_of(kb * tk, tk), tk), :]
        for c in range(2):
            qc = q[lo:, c * HEAD_DIM:(c + 1) * HEAD_DIM]
            kc = kblk[:, c * HEAD_DIM:(c + 1) * HEAD_DIM]
            s_scr[c, :, lo:] = lax.dot_general(kc, qc, (((1,), (1,)), ((), ())),
                                               preferred_element_type=F32)

    def consume(kb, s_scr, lo=0, masked=False):
        vblk = v_ref[pl.ds(pl.multiple_of(kb * tk, tk), tk), :]
        for c in range(2):
            s = s_scr[c, :, lo:]
            if masked:
                s = jnp.where(causal[:, :tq - lo], s, NEG_BIG)
            m_prev = m_scr[c, :, lo:]
            m_new = jnp.maximum(m_prev, jnp.max(s, axis=0, keepdims=True))
            p = jnp.exp2(s - m_new)
            alpha = jnp.exp2(m_prev - m_new)
            l_scr[c, :, lo:] = alpha * l_scr[c, :, lo:] + jnp.sum(p, axis=0, keepdims=True)
            pv = lax.dot_general(vblk, p.astype(BF16), (((0,), (0,)), ((), ())),
                                 preferred_element_type=F32)
            acc_scr[c, :, lo:] = alpha * acc_scr[c, :, lo:] + pv
            m_scr[c, :, lo:] = m_new

    scores(0, sa_scr)

    def pair(p, carry):
        scores(2 * p + 1, sb_scr)
        consume(2 * p, sa_scr)
        scores(2 * p + 2, sa_scr)
        consume(2 * p + 1, sb_scr)
        return carry

    lax.fori_loop(0, qi, pair, 0)
    scores(2 * qi + 1, sb_scr, lo=tk)
    consume(2 * qi, sa_scr, masked=True)
    consume(2 * qi + 1, sb_scr, lo=tk, masked=True)

    lv = lam_ref[...]
    lam = (jnp.exp(jnp.sum(lv[0:1] * lv[1:2], axis=-1, keepdims=True))
           - jnp.exp(jnp.sum(lv[2:3] * lv[3:4], axis=-1, keepdims=True)) + lambda_init)
    o = acc_scr[0] / l_scr[0] - lam * (acc_scr[1] / l_scr[1])
    ms = jnp.mean(o * o, axis=0, keepdims=True)
    o = (o * lax.rsqrt(ms + NORM_EPS)).T
    o_ref[...] = ((o * sub_ref[...]) * (1.0 - lambda_init)).astype(BF16)


def _attn(qkv, lam, subln, *, j, B, S, D, tq, lambda_init):
    H = D // V_DIM
    tq = min(tq, S)
    nq = S // tq
    vmem = 2 * 2 * S * V_DIM * 2 + 4 * tq * V_DIM * 2 + 2 * tq * V_DIM * 4 \
        + 4 * tq * V7X_SUBLANES * 4 + 6 * tq * tq * 4 + (6 << 20)
    return pl.pallas_call(
        functools.partial(_attn_body, tq=tq, lambda_init=lambda_init),
        out_shape=jax.ShapeDtypeStruct((B * S, D), BF16),
        grid=(B, H, nq),
        in_specs=[
            pl.BlockSpec((tq, V_DIM), lambda b, h, qi: (b * nq + qi, h)),
            pl.BlockSpec((S, V_DIM), lambda b, h, qi: (b, H + h)),
            pl.BlockSpec((S, V_DIM), lambda b, h, qi: (b, 2 * H + h)),
            pl.BlockSpec((None, 4, HEAD_DIM), lambda b, h, qi: (j, 0, 0)),
            pl.BlockSpec((None, 1, V_DIM), lambda b, h, qi: (j, 0, 0)),
        ],
        out_specs=pl.BlockSpec((tq, V_DIM), lambda b, h, qi: (b * nq + qi, h)),
        scratch_shapes=[pltpu.VMEM((2, 1, tq), F32), pltpu.VMEM((2, 1, tq), F32),
                        pltpu.VMEM((2, V_DIM, tq), F32),
                        pltpu.VMEM((2, tq // 2, tq), F32), pltpu.VMEM((2, tq // 2, tq), F32)],
        compiler_params=_compiler_params(3, vmem),
        name="diff_attn",
    )(qkv, qkv, qkv, lam, subln.reshape(-1, 1, V_DIM))


def _oproj_body(x_ref, a_ref, w_ref, mod_ref, o_ref):
    m = jnp.dot(a_ref[...], w_ref[...], preferred_element_type=F32)
    o_ref[...] = x_ref[...] + mod_ref[5:6, :] * m


def _oproj(xv, a, w, mod, *, layer, j, B, S, tm):
    D = xv.shape[-1]
    tm = min(tm, S)
    nt = S // tm
    rows = lambda b, ti: (b * nt + ti, 0)
    vmem = 4 * tm * D * 4 + 2 * tm * D * 2 + D * D * 2 + tm * D * 4 + (6 << 20)
    return pl.pallas_call(
        _oproj_body,
        out_shape=jax.ShapeDtypeStruct((B * S, D), F32),
        grid=(B, nt),
        in_specs=[
            pl.BlockSpec((tm, D), rows),
            pl.BlockSpec((tm, D), rows),
            _resident((None, D, D), lambda b, ti: (j, 0, 0)),
            pl.BlockSpec((None, None, N_SUBLAYERS * 3, D), lambda b, ti: (layer, b, 0, 0)),
        ],
        out_specs=pl.BlockSpec((tm, D), rows),
        compiler_params=_compiler_params(2, vmem),
        name=f"attn_out_l{layer}",
    )(xv, a, w, mod)


def _ssm_prep_body(ar_ref, ai_ref, ls_ref, arr_ref, air_ref, lsr_ref, br_ref, bi_ref,
                   abr_ref, abi_ref, a2r_ref, a2i_ref, bbr_ref, bbi_ref, abbr_ref, abbi_ref):
    def abar(ar, ai, ls):
        dt = jnp.exp(ls)
        mag = jnp.exp(dt * ar)
        return mag * jnp.cos(dt * ai), mag * jnp.sin(dt * ai)

    re, im = abar(ar_ref[...], ai_ref[...], ls_ref[...])
    abr_ref[...] = re
    abi_ref[...] = im
    a2r_ref[...] = re * re - im * im
    a2i_ref[...] = 2.0 * (re * im)

    ar, ai = arr_ref[...], air_ref[...]
    re, im = abar(ar, ai, lsr_ref[...])
    den = ar * ar + ai * ai
    num_re = re - 1.0
    coef_re = (num_re * ar + im * ai) / den
    coef_im = (im * ar - num_re * ai) / den
    br, bi = br_ref[...], bi_ref[...]
    bbr = coef_re * br - coef_im * bi
    bbi = coef_re * bi + coef_im * br
    bbr_ref[...] = bbr
    bbi_ref[...] = bbi
    abbr_ref[...] = re * bbr - im * bbi
    abbi_ref[...] = re * bbi + im * bbr


def _ssm_prep(a_re, a_im, log_step, b_re, b_im):
    G, N = a_re.shape
    P = b_re.shape[-1]
    rep = lambda v: jnp.repeat(v, P, axis=0)
    flat = lambda v: jnp.swapaxes(v, 1, 2).reshape(G * P, N)
    ls = log_step.reshape(G, 1)
    small = jax.ShapeDtypeStruct((G, N), F32)
    big = jax.ShapeDtypeStruct((G * P, N), F32)
    return pl.pallas_call(
        _ssm_prep_body,
        out_shape=(small, small, small, small, big, big, big, big),
        name="ssm_discretise",
    )(a_re, a_im, ls, rep(a_re), rep(a_im), rep(ls), flat(b_re), flat(b_im))


def _ssm_block_operands(abr, abi, a2r, a2i, bbr, bbi, abbr, abbi, c_re, c_im):
    G, N = abr.shape
    P = SSM_GROUP
    gb = SSM_GROUPS_PER_BLOCK
    nb = G // gb
    eye = jnp.eye(gb, dtype=F32)

    def b_block(v):
        v = v.reshape(nb, gb, P, 1, N) * eye[None, :, None, :, None]
        return v.reshape(nb, gb * P, gb * N)

    def c_block(v):
        v = jnp.swapaxes(v.reshape(nb, gb, P, N), 2, 3)
        v = v.reshape(nb, gb, N, 1, P) * eye[None, :, None, :, None]
        return v.reshape(nb, gb * N, gb * P)

    bcur = jnp.concatenate([b_block(bbr), b_block(bbi)], axis=-1)
    bprev = jnp.concatenate([b_block(abbr), b_block(abbi)], axis=-1)
    bcat = jnp.concatenate([bcur, bprev], axis=1).astype(BF16)

    c8 = jnp.concatenate([c_block(c_re), -c_block(c_im)], axis=1)
    npair = SSM_BLOCKS_PER_READOUT
    eye2 = jnp.eye(npair, dtype=F32)
    cpair = c8.reshape(nb // npair, npair, 2 * gb * N, 1, gb * P) * eye2[None, :, None, :, None]
    cpair = cpair.reshape(nb // npair, npair * 2 * gb * N, npair * gb * P).astype(BF16)

    first = (lax.broadcasted_iota(jnp.int32, (1, 1, V7X_SUBLANES, 1), 2) < V7X_SUBLANES // 2)
    a1 = jnp.stack([abr, abi], axis=0).reshape(2, nb, 1, gb * N).swapaxes(0, 1)
    a2 = jnp.stack([a2r, a2i], axis=0).reshape(2, nb, 1, gb * N).swapaxes(0, 1)
    a_carry = jnp.where(first, a1, a2)
    return bcat, cpair, a_carry


def _gelu_tanh(x):
    c = math.sqrt(2.0 / math.pi)
    return x * (0.5 * (1.0 + jnp.tanh(c * (x + 0.044715 * (x * x * x)))))


def _ssm_body(x_ref, mod_ref, g_ref, d_ref, bcat_ref, cpair_ref, acar_ref, z_ref,
              hs_scr, hcat_scr, st_scr, y_scr, carry_scr, *, tt, nb):
    i = pl.program_id(0)
    B, _, D = x_ref.shape
    tm = tt * B
    sub = V7X_SUBLANES
    ns = SSM_BLOCK_STATES
    kin = 2 * V7X_LANES
    npair = SSM_BLOCKS_PER_READOUT
    cout = npair * V7X_LANES
    slab = lambda k: slice(k * V7X_LANES, (k + 1) * V7X_LANES)

    @pl.when(i == 0)
    def _():
        carry_scr[...] = jnp.zeros(carry_scr.shape, F32)

    for b in range(B):
        h = _norm_mod(x_ref[b], g_ref[...], mod_ref[b, 3:4, :], mod_ref[b, 4:5, :])
        for k in range(nb):
            hs_scr[k, pl.ds(b, tt, stride=B), :] = h[:, slab(k)]
    second = (lax.broadcasted_iota(jnp.int32, (tm, V7X_LANES), 0) & (sub // 2)) != 0
    for k in range(nb):
        hk = hs_scr[k]
        hcat_scr[:, k * kin:k * kin + V7X_LANES] = hk.astype(BF16)
        hcat_scr[:, k * kin + V7X_LANES:(k + 1) * kin] = jnp.where(
            second, pltpu.roll(hk, sub // 2, 0), 0.0).astype(BF16)

    first = lax.broadcasted_iota(jnp.int32, (sub, ns), 0) < sub // 2
    re_sl, im_sl = slice(0, ns), slice(ns, 2 * ns)

    for sb in range(nb):
        buf = sb % SSM_STATE_BUFFERS
        st_scr[buf] = jnp.dot(hcat_scr[:, sb * kin:(sb + 1) * kin], bcat_ref[sb],
                              preferred_element_type=F32)
        ar, ai = acar_ref[sb, 0], acar_ref[sb, 1]

        def step(r, carry, buf=buf, ar=ar, ai=ai):
            hr, hi = carry
            rows = pl.ds(pl.multiple_of(r * sub, sub), sub)
            sr = st_scr[buf, rows, re_sl] + (hr * ar - hi * ai)
            si = st_scr[buf, rows, im_sl] + (hr * ai + hi * ar)
            st_scr[buf, rows, re_sl] = sr
            st_scr[buf, rows, im_sl] = si
            return (jnp.where(first, pltpu.roll(sr, sub // 2, 0), sr),
                    jnp.where(first, pltpu.roll(si, sub // 2, 0), si))

        hr, hi = lax.fori_loop(0, tm // sub, step, (carry_scr[sb, 0], carry_scr[sb, 1]),
                               unroll=True)
        carry_scr[sb, 0] = hr
        carry_scr[sb, 1] = hi

        if sb % npair == npair - 1:
            pair = sb // npair
            acc = None
            for e in range(npair):
                states = st_scr[(sb - npair + 1 + e) % SSM_STATE_BUFFERS].astype(BF16)
                part = jnp.dot(states, cpair_ref[pair, e * 2 * ns:(e + 1) * 2 * ns, :],
                               preferred_element_type=F32)
                acc = part if acc is None else acc + part
            y_scr[:, pair * cout:(pair + 1) * cout] = acc

    for k in range(nb):
        hs_scr[k] = _gelu_tanh(y_scr[:, slab(k)] + d_ref[:, slab(k)] * hs_scr[k])
    for b in range(B):
        for k in range(nb):
            z_ref[b, :, slab(k)] = hs_scr[k, pl.ds(b, tt, stride=B), :].astype(BF16)


def _ssm(xv, mod, norm_g, d_skip, bcat, cpair, a_carry, *, layer, j, B, S, tt):
    D = norm_g.shape[-1]
    assert V7X_SUBLANES == 2 * B, "the scan packs 2 timesteps x B batches per 8-sublane row-group"
    nb = bcat.shape[0]
    assert nb == D // V7X_LANES
    tt = min(tt, S)
    tm = tt * B
    ns2 = 2 * SSM_BLOCK_STATES
    const = lambda *shape: _resident(shape, lambda i: (0,) * len(shape))
    vmem = 2 * tm * D * 4 + 2 * tm * D * 2 + 2 * tm * D * 4 + 2 * tm * D * 2 \
        + (SSM_STATE_BUFFERS + 2) * tm * ns2 * 4 + (bcat.size + cpair.size) * 2 + 2 * tm * D * 4 \
        + (6 << 20)
    return pl.pallas_call(
        functools.partial(_ssm_body, tt=tt, nb=nb),
        out_shape=jax.ShapeDtypeStruct((B, S, D), BF16),
        grid=(S // tt,),
        in_specs=[
            pl.BlockSpec((B, tt, D), lambda i: (0, i, 0)),
            pl.BlockSpec((None, B, N_SUBLAYERS * 3, D), lambda i: (layer, 0, 0, 0)),
            pl.BlockSpec((None, None, 1, D), lambda i: (layer, 1, 0, 0)),
            pl.BlockSpec((None, 1, D), lambda i: (j, 0, 0)),
            const(*bcat.shape),
            const(*cpair.shape),
            const(*a_carry.shape),
        ],
        out_specs=pl.BlockSpec((B, tt, D), lambda i: (0, i, 0)),
        scratch_shapes=[
            pltpu.VMEM((nb, tm, V7X_LANES), F32),
            pltpu.VMEM((tm, 2 * D), BF16),
            pltpu.VMEM((SSM_STATE_BUFFERS, tm, ns2), F32),
            pltpu.VMEM((tm, D), F32),
            pltpu.VMEM((nb, 2, V7X_SUBLANES, SSM_BLOCK_STATES), F32),
        ],
        compiler_params=_compiler_params(1, vmem),
        name=f"ssm_l{layer}",
    )(xv.reshape(B, S, D), mod, norm_g.reshape(norm_g.shape[0], N_SUBLAYERS, 1, D),
      d_skip.reshape(-1, 1, D), bcat, cpair, a_carry)


def _glu_body(x_ref, z_ref, wv_ref, wg_ref, mod_ref, o_ref):
    z = z_ref[...]
    val = jnp.dot(z, wv_ref[...], preferred_element_type=F32)
    gt = jnp.dot(z, wg_ref[...], preferred_element_type=F32)
    o_ref[...] = x_ref[...] + mod_ref[5:6, :] * (val * jax.nn.sigmoid(gt))


def _glu(xv, zv, w, mod, *, layer, j, B, S, tm):
    D, tn = w.shape[-2:]
    tm = min(tm, S)
    nt, nn = S // tm, D // tn
    vmem = 4 * tm * tn * 4 + 2 * tm * D * 2 + 4 * D * tn * 2 + 3 * tm * tn * 4 + (6 << 20)
    return pl.pallas_call(
        _glu_body,
        out_shape=jax.ShapeDtypeStruct((B * S, D), F32),
        grid=(B, nt, nn),
        in_specs=[
            pl.BlockSpec((tm, tn), lambda b, ti, jn: (b * nt + ti, jn)),
            pl.BlockSpec((tm, D), lambda b, ti, jn: (b * nt + ti, 0)),
            pl.BlockSpec((None, None, D, tn), lambda b, ti, jn: (j, jn, 0, 0)),
            pl.BlockSpec((None, None, D, tn), lambda b, ti, jn: (j, nn + jn, 0, 0)),
            pl.BlockSpec((None, None, N_SUBLAYERS * 3, tn), lambda b, ti, jn: (layer, b, 0, jn)),
        ],
        out_specs=pl.BlockSpec((tm, tn), lambda b, ti, jn: (b * nt + ti, jn)),
        compiler_params=_compiler_params(3, vmem),
        name=f"glu_l{layer}",
    )(xv, zv, w, w, mod)


def kernel(x, c, positions, norm_g, ada_w, ada_b, ffn_w_in, ffn_w_out, attn_w_in, attn_w_out,
           attn_q_norm, attn_k_norm, attn_lambda, attn_subln, ssm_a_re, ssm_a_im, ssm_log_step,
           ssm_b_re, ssm_b_im, ssm_c_re, ssm_c_im, ssm_d, ssm_w_glu):
    B, S, D = x.shape
    depth = norm_g.shape[0]

    mod = _adaln(c, ada_w, ada_b)
    ffn_w_in = _column_tiled(ffn_w_in.astype(BF16), 512)
    ffn_w_out = ffn_w_out.astype(BF16)
    attn_w_in = attn_w_in.astype(BF16)
    attn_w_out = attn_w_out.astype(BF16)
    ssm_w_glu = _column_tiled(ssm_w_glu.astype(BF16), 1024)
    pos = positions.reshape(B * S, 1)

    ffn = functools.partial(_ffn, mod=mod, norm_g=norm_g, w_in=ffn_w_in, w_out=ffn_w_out,
                            B=B, S=S, tm=1024)
    xv = x.reshape(B * S, D)
    for i in range(depth):
        j = i // N_MIXERS
        xv = ffn(xv, layer=i, sl=0, which=0)
        if i % N_MIXERS == 0:
            lambda_init = 0.8 - 0.6 * math.exp(-0.3 * i)
            qkv = _qkv(xv, pos, mod, norm_g, attn_w_in, attn_q_norm, attn_k_norm,
                       layer=i, j=j, B=B, S=S, tm=512, tn=512)
            att = _attn(qkv, attn_lambda, attn_subln, j=j, B=B, S=S, D=D, tq=1024,
                        lambda_init=lambda_init)
            xv = _oproj(xv, att, attn_w_out, mod, layer=i, j=j, B=B, S=S, tm=512)
        else:
            disc = _ssm_prep(ssm_a_re[j], ssm_a_im[j], ssm_log_step[j], ssm_b_re[j], ssm_b_im[j])
            operands = _ssm_block_operands(*disc, ssm_c_re[j], ssm_c_im[j])
            z = _ssm(xv, mod, norm_g, ssm_d, *operands, layer=i, j=j, B=B, S=S, tt=64)
            xv = _glu(xv, z.reshape(B * S, D), ssm_w_glu, mod, layer=i, j=j, B=B, S=S,
                      tm=512)
        xv = ffn(xv, layer=i, sl=2, which=1)
    return xv.reshape(B, S, D)
```

```python
import functools
import math

import numpy as np
import jax
import jax.numpy as jnp
from jax import lax
from jax.experimental import pallas as pl
from jax.experimental.pallas import tpu as pltpu

HEAD_DIM = 128
V_DIM = 2 * HEAD_DIM
ROT_DIM = HEAD_DIM // 4
ROPE_THETA = 500000.0
SSM_GROUP = 16
SSM_STATE = 64
N_SUBLAYERS = 3
N_MIXERS = 2
NORM_EPS = 1e-6
MACARON_WEIGHT = 0.5

V7X_LANES = 128
V7X_SUBLANES = 8
V7X_MXU_DIM = 256
V7X_VMEM_BYTES = 64 * 1024 * 1024
V7X_SCOPED_VMEM_CAP = 58 * 1024 * 1024

F32 = jnp.float32
BF16 = jnp.bfloat16
NEG_BIG = -1e30

SSM_GROUPS_PER_BLOCK = V7X_LANES // SSM_GROUP
SSM_BLOCK_STATES = SSM_GROUPS_PER_BLOCK * SSM_STATE
SSM_BLOCKS_PER_READOUT = V7X_MXU_DIM // V7X_LANES
SSM_STATE_BUFFERS = 2 * SSM_BLOCKS_PER_READOUT

FFN_FIRST_STEP_CHUNKS = 4
FFN_NORM_ROWS = 64


def _compiler_params(n_axes, vmem_bytes, flags=None):
    limit = min(int(vmem_bytes), V7X_SCOPED_VMEM_CAP)
    return pltpu.CompilerParams(dimension_semantics=("arbitrary",) * n_axes,
                                vmem_limit_bytes=limit, flags=flags)


def _resident(block_shape, index_map):
    return pl.BlockSpec(block_shape, index_map, pipeline_mode=pl.Buffered(1))


def _norm_mod(x, g, shift, scale):
    ms = jnp.mean(x * x, axis=-1, keepdims=True)
    y = x * lax.rsqrt(ms + NORM_EPS) * g
    return y * (1.0 + scale) + shift


def _adaln_body(c_ref, w_ref, b_ref, o_ref):
    c = c_ref[...]
    cond = (c * jax.nn.sigmoid(c)).astype(BF16)
    o_ref[...] = jnp.dot(cond, w_ref[...].astype(BF16), preferred_element_type=F32) + b_ref[...]


def _adaln(c, ada_w, ada_b):
    L, D, N = ada_w.shape
    B = c.shape[0]
    rows = V7X_SUBLANES * ((B + V7X_SUBLANES - 1) // V7X_SUBLANES)
    c_pad = jnp.pad(c, ((0, rows - B), (0, 0)))
    tn = D
    out = pl.pallas_call(
        _adaln_body,
        out_shape=jax.ShapeDtypeStruct((L, rows, N), F32),
        grid=(L, N // tn),
        in_specs=[
            pl.BlockSpec((rows, D), lambda l, j: (0, 0)),
            pl.BlockSpec((None, D, tn), lambda l, j: (l, 0, j)),
            pl.BlockSpec((None, 1, tn), lambda l, j: (l, 0, j)),
        ],
        out_specs=pl.BlockSpec((None, rows, tn), lambda l, j: (l, 0, j)),
        compiler_params=_compiler_params(2, 2 * D * tn * 4 + D * tn * 2 + (8 << 20)),
        name="adaln",
    )(c_pad, ada_w, ada_b.reshape(L, 1, N))
    return out[:, :B].reshape(L, B, N_SUBLAYERS * 3, D)


def _ffn_body(*refs, sl, cast_next):
    if cast_next:
        (x_ref, mod_ref, g_ref, wa_ref, wb_ref, wo_ref, nwi_ref, nwo_ref,
         o_ref, nwi_out_ref, nwo_out_ref, h_scr) = refs
        nwi_out_ref[...] = nwi_ref[...].astype(BF16)
        nwo_out_ref[...] = nwo_ref[...].astype(BF16)
    else:
        x_ref, mod_ref, g_ref, wa_ref, wb_ref, wo_ref, o_ref, h_scr = refs
    j = pl.program_id(2)
    nj = pl.num_programs(2)
    tm = x_ref.shape[0]

    def swiglu(h):
        a = jnp.dot(h, wa_ref[...], preferred_element_type=F32)
        b = jnp.dot(h, wb_ref[...], preferred_element_type=F32)
        act = (a * jax.nn.sigmoid(a) * b).astype(BF16)
        return jnp.dot(act, wo_ref[...], preferred_element_type=F32)

    @pl.when(j == 0)
    def _():
        shift, scale = mod_ref[3 * sl:3 * sl + 1, :], mod_ref[3 * sl + 1:3 * sl + 2, :]
        rc = tm // FFN_FIRST_STEP_CHUNKS
        nr = min(FFN_NORM_ROWS, rc)
        for r in range(FFN_FIRST_STEP_CHUNKS):
            for lo in range(r * rc, (r + 1) * rc, nr):
                h_scr[lo:lo + nr, :] = _norm_mod(x_ref[lo:lo + nr, :], g_ref[...], shift,
                                                 scale).astype(BF16)
            o_ref[r * rc:(r + 1) * rc, :] = swiglu(h_scr[r * rc:(r + 1) * rc, :])

    @pl.when(j > 0)
    def _():
        o_ref[...] += swiglu(h_scr[...])

    @pl.when(j == nj - 1)
    def _():
        gate = mod_ref[3 * sl + 2:3 * sl + 3, :]
        o_ref[...] = x_ref[...] + (MACARON_WEIGHT * gate) * o_ref[...]


def _ffn(xv, mod, norm_g, w_in, w_out, next_w, *, layer, sl, B, S, tm, tf):
    D = norm_g.shape[-1]
    F = w_out.shape[0]
    tm = min(tm, S)
    nt, nj = S // tm, F // tf
    rows = lambda b, ti, j: (b * nt + ti, 0)
    in_specs = [
        pl.BlockSpec((tm, D), rows),
        pl.BlockSpec((None, None, N_SUBLAYERS * 3, D), lambda b, ti, j: (layer, b, 0, 0)),
        pl.BlockSpec((None, None, 1, D), lambda b, ti, j: (layer, sl, 0, 0)),
        pl.BlockSpec((D, tf), lambda b, ti, j: (0, j)),
        pl.BlockSpec((D, tf), lambda b, ti, j: (0, nj + j)),
        pl.BlockSpec((tf, D), lambda b, ti, j: (j, 0)),
    ]
    operands = [xv, mod, norm_g.reshape(norm_g.shape[0], N_SUBLAYERS, 1, D), w_in, w_in, w_out]
    out_specs = [pl.BlockSpec((tm, D), rows)]
    out_shape = [jax.ShapeDtypeStruct((B * S, D), F32)]
    if next_w is not None:
        nwi, nwo, l2, w2 = next_w
        steps = B * nt * nj
        assert D % (B * nt) == 0 and (2 * F) % nj == 0 and F % steps == 0
        rin, cin, rout = D // (B * nt), 2 * F // nj, F // steps
        in_specs += [
            pl.BlockSpec((None, None, rin, cin), lambda b, ti, j: (l2, w2, b * nt + ti, j)),
            pl.BlockSpec((None, None, rout, D), lambda b, ti, j: (l2, w2, (b * nt + ti) * nj + j, 0)),
        ]
        operands += [nwi, nwo]
        out_specs += [
            pl.BlockSpec((rin, cin), lambda b, ti, j: (b * nt + ti, j)),
            pl.BlockSpec((rout, D), lambda b, ti, j: ((b * nt + ti) * nj + j, 0)),
        ]
        out_shape += [jax.ShapeDtypeStruct((D, 2 * F), BF16), jax.ShapeDtypeStruct((F, D), BF16)]
    vmem = 4 * tm * D * 4 + tm * D * 2 + 2 * 3 * D * tf * 2 + 4 * tm * tf * 4 + (6 << 20)
    out = pl.pallas_call(
        functools.partial(_ffn_body, sl=sl, cast_next=next_w is not None),
        out_shape=out_shape,
        grid=(B, nt, nj),
        in_specs=in_specs,
        out_specs=out_specs,
        scratch_shapes=[pltpu.VMEM((tm, D), BF16)],
        compiler_params=_compiler_params(3, vmem),
        name=f"ffn_l{layer}_s{sl}",
    )(*operands)
    return out[0], tuple(out[1:])


def _qkv_body(x_ref, pos_ref, invf_ref, mod_ref, g_ref, w_ref, gain_ref, o_ref, *, tn, q_mult):
    D = x_ref.shape[-1]
    half = ROT_DIM // 2
    h = _norm_mod(x_ref[...], g_ref[...], mod_ref[3:4, :], mod_ref[4:5, :]).astype(BF16)

    ang = pos_ref[...].astype(F32) * invf_ref[...]
    c, s = jnp.cos(ang), jnp.sin(ang)
    lane = lax.broadcasted_iota(jnp.int32, ang.shape, 1)
    cos = jnp.where(lane < ROT_DIM, c, 1.0)
    sin_a = jnp.where(lane < half, -s, 0.0)
    sin_b = jnp.where((lane >= half) & (lane < ROT_DIM), s, 0.0)
    gains = (gain_ref[0] * q_mult, gain_ref[1])

    for jn in range(w_ref.shape[-1] // tn):
        cols = slice(jn * tn, (jn + 1) * tn)
        acc = jnp.dot(h, w_ref[:, cols], preferred_element_type=F32)
        which = (jn * tn) // D
        if which == 2:
            o_ref[:, cols] = acc.astype(BF16)
            continue
        for ch in range(tn // HEAD_DIM):
            t = acc[:, ch * HEAD_DIM:(ch + 1) * HEAD_DIM]
            ms = jnp.mean(t * t, axis=-1, keepdims=True)
            y = t * lax.rsqrt(ms + NORM_EPS) * gains[which]
            r = (y * cos + pltpu.roll(y, HEAD_DIM - half, 1) * sin_a
                 + pltpu.roll(y, half, 1) * sin_b)
            lo = jn * tn + ch * HEAD_DIM
            o_ref[:, lo:lo + HEAD_DIM] = r.astype(BF16)


def _qkv(xv, pos, mod, norm_g, w, q_gain, k_gain, *, layer, j, B, S, tm, tn):
    D = norm_g.shape[-1]
    N = w.shape[-1]
    tm = min(tm, S)
    nt = S // tm
    half = ROT_DIM // 2
    inv_freq = ROPE_THETA ** (-np.arange(0, ROT_DIM, 2, dtype=np.float64) / ROT_DIM)
    invf = jnp.asarray(np.tile(inv_freq, HEAD_DIM // half)[None, :], dtype=F32)
    gains = jnp.stack([q_gain, k_gain], axis=1).reshape(-1, 2, 1, HEAD_DIM)
    q_mult = HEAD_DIM ** -0.5 * math.log2(math.e)
    rows = lambda b, ti: (b * nt + ti, 0)
    vmem = 2 * tm * D * 4 + tm * D * 2 + D * N * 2 + 2 * tm * N * 2 + 4 * tm * tn * 4 \
        + 8 * tm * V7X_LANES * 4 + (6 << 20)
    return pl.pallas_call(
        functools.partial(_qkv_body, tn=tn, q_mult=q_mult),
        out_shape=jax.ShapeDtypeStruct((B * S, N), BF16),
        grid=(B, nt),
        in_specs=[
            pl.BlockSpec((tm, D), rows),
            pl.BlockSpec((tm, 1), rows),
            pl.BlockSpec((1, HEAD_DIM), lambda b, ti: (0, 0)),
            pl.BlockSpec((None, None, N_SUBLAYERS * 3, D), lambda b, ti: (layer, b, 0, 0)),
            pl.BlockSpec((None, None, 1, D), lambda b, ti: (layer, 1, 0, 0)),
            _resident((None, D, N), lambda b, ti: (j, 0, 0)),
            pl.BlockSpec((None, 2, 1, HEAD_DIM), lambda b, ti: (j, 0, 0, 0)),
        ],
        out_specs=pl.BlockSpec((tm, N), rows),
        compiler_params=_compiler_params(2, vmem),
        name=f"qkv_l{layer}",
    )(xv, pos, invf, mod, norm_g.reshape(norm_g.shape[0], N_SUBLAYERS, 1, D), w, gains)


def _attn_body(q_ref, k_ref, v_ref, lam_ref, sub_ref, o_ref, m_scr, l_scr, acc_scr,
               sa_scr, sb_scr, *, tq, lambda_init):
    qi = pl.program_id(2)
    tk = tq // 2
    m_scr[...] = jnp.full(m_scr.shape, NEG_BIG, F32)
    l_scr[...] = jnp.zeros(l_scr.shape, F32)
    acc_scr[...] = jnp.zeros(acc_scr.shape, F32)
    q = q_ref[...]
    key = lax.broadcasted_iota(jnp.int32, (tk, tq), 0)
    qry = lax.broadcasted_iota(jnp.int32, (tk, tq), 1)
    causal = key <= qry

    def scores(kb, s_scr, lo=0):
        kblk = k_ref[pl.ds(pl.multiple_of(kb * tk, tk), tk), :]
        for c in range(2):
            qc = q[lo:, c * HEAD_DIM:(c + 1) * HEAD_DIM]
            kc = kblk[:, c * HEAD_DIM:(c + 1) * HEAD_DIM]
            s_scr[c, :, lo:] = lax.dot_general(kc, qc, (((1,), (1,)), ((), ())),
                                               preferred_element_type=F32)

    def consume(kb, s_scr, lo=0, masked=False):
        vblk = v_ref[pl.ds(pl.multiple_of(kb * tk, tk), tk), :]
        for c in range(2):
            s = s_scr[c, :, lo:]
            if masked:
                s = jnp.where(causal[:, :tq - lo], s, NEG_BIG)
            m_prev = m_scr[c, :, lo:]
            m_new = jnp.maximum(m_prev, jnp.max(s, axis=0, keepdims=True))
            p = jnp.exp2(s - m_new)
            alpha = jnp.exp2(m_prev - m_new)
            l_scr[c, :, lo:] = alpha * l_scr[c, :, lo:] + jnp.sum(p, axis=0, keepdims=True)
            pv = lax.dot_general(vblk, p.astype(BF16), (((0,), (0,)), ((), ())),
                                 preferred_element_type=F32)
            acc_scr[c, :, lo:] = alpha * acc_scr[c, :, lo:] + pv
            m_scr[c, :, lo:] = m_new

    scores(0, sa_scr)

    def pair(p, carry):
        scores(2 * p + 1, sb_scr)
        consume(2 * p, sa_scr)
        scores(2 * p + 2, sa_scr)
        consume(2 * p + 1, sb_scr)
        return carry

    lax.fori_loop(0, qi, pair, 0)
    scores(2 * qi + 1, sb_scr, lo=tk)
    consume(2 * qi, sa_scr, masked=True)
    consume(2 * qi + 1, sb_scr, lo=tk, masked=True)

    lv = lam_ref[...]
    lam = (jnp.exp(jnp.sum(lv[0:1] * lv[1:2], axis=-1, keepdims=True))
           - jnp.exp(jnp.sum(lv[2:3] * lv[3:4], axis=-1, keepdims=True)) + lambda_init)
    o = acc_scr[0] / l_scr[0] - lam * (acc_scr[1] / l_scr[1])
    ms = jnp.mean(o * o, axis=0, keepdims=True)
    o = (o * lax.rsqrt(ms + NORM_EPS)).T
    o_ref[...] = ((o * sub_ref[...]) * (1.0 - lambda_init)).astype(BF16)


def _attn(qkv, lam, subln, *, j, B, S, D, tq, lambda_init):
    H = D // V_DIM
    tq = min(tq, S)
    nq = S // tq
    vmem = 2 * 2 * S * V_DIM * 2 + 4 * tq * V_DIM * 2 + 2 * tq * V_DIM * 4 \
        + 4 * tq * V7X_SUBLANES * 4 + 6 * tq * tq * 4 + (6 << 20)
    return pl.pallas_call(
        functools.partial(_attn_body, tq=tq, lambda_init=lambda_init),
        out_shape=jax.ShapeDtypeStruct((B * S, D), BF16),
        grid=(B, H, nq),
        in_specs=[
            pl.BlockSpec((tq, V_DIM), lambda b, h, qi: (b * nq + qi, h)),
            pl.BlockSpec((S, V_DIM), lambda b, h, qi: (b, H + h)),
            pl.BlockSpec((S, V_DIM), lambda b, h, qi: (b, 2 * H + h)),
            pl.BlockSpec((None, 4, HEAD_DIM), lambda b, h, qi: (j, 0, 0)),
            pl.BlockSpec((None, 1, V_DIM), lambda b, h, qi: (j, 0, 0)),
        ],
        out_specs=pl.BlockSpec((tq, V_DIM), lambda b, h, qi: (b * nq + qi, h)),
        scratch_shapes=[pltpu.VMEM((2, 1, tq), F32), pltpu.VMEM((2, 1, tq), F32),
                        pltpu.VMEM((2, V_DIM, tq), F32),
                        pltpu.VMEM((2, tq // 2, tq), F32), pltpu.VMEM((2, tq // 2, tq), F32)],
        compiler_params=_compiler_params(3, vmem),
        name="diff_attn",
    )(qkv, qkv, qkv, lam, subln.reshape(-1, 1, V_DIM))


def _oproj_body(x_ref, a_ref, w_ref, mod_ref, o_ref):
    m = jnp.dot(a_ref[...], w_ref[...], preferred_element_type=F32)
    o_ref[...] = x_ref[...] + mod_ref[5:6, :] * m


def _oproj(xv, a, w, mod, *, layer, j, B, S, tm):
    D = xv.shape[-1]
    tm = min(tm, S)
    nt = S // tm
    rows = lambda b, ti: (b * nt + ti, 0)
    vmem = 4 * tm * D * 4 + 2 * tm * D * 2 + D * D * 2 + tm * D * 4 + (6 << 20)
    return pl.pallas_call(
        _oproj_body,
        out_shape=jax.ShapeDtypeStruct((B * S, D), F32),
        grid=(B, nt),
        in_specs=[
            pl.BlockSpec((tm, D), rows),
            pl.BlockSpec((tm, D), rows),
            _resident((None, D, D), lambda b, ti: (j, 0, 0)),
            pl.BlockSpec((None, None, N_SUBLAYERS * 3, D), lambda b, ti: (layer, b, 0, 0)),
        ],
        out_specs=pl.BlockSpec((tm, D), rows),
        compiler_params=_compiler_params(2, vmem),
        name=f"attn_out_l{layer}",
    )(xv, a, w, mod)


def _ssm_prep_body(ar_ref, ai_ref, ls_ref, arr_ref, air_ref, lsr_ref, br_ref, bi_ref,
                   abr_ref, abi_ref, a2r_ref, a2i_ref, bbr_ref, bbi_ref, abbr_ref, abbi_ref):
    def abar(ar, ai, ls):
        dt = jnp.exp(ls)
        mag = jnp.exp(dt * ar)
        return mag * jnp.cos(dt * ai), mag * jnp.sin(dt * ai)

    re, im = abar(ar_ref[...], ai_ref[...], ls_ref[...])
    abr_ref[...] = re
    abi_ref[...] = im
    a2r_ref[...] = re * re - im * im
    a2i_ref[...] = 2.0 * (re * im)

    ar, ai = arr_ref[...], air_ref[...]
    re, im = abar(ar, ai, lsr_ref[...])
    den = ar * ar + ai * ai
    num_re = re - 1.0
    coef_re = (num_re * ar + im * ai) / den
    coef_im = (im * ar - num_re * ai) / den
    br, bi = br_ref[...], bi_ref[...]
    bbr = coef_re * br - coef_im * bi
    bbi = coef_re * bi + coef_im * br
    bbr_ref[...] = bbr
    bbi_ref[...] = bbi
    abbr_ref[...] = re * bbr - im * bbi
    abbi_ref[...] = re * bbi + im * bbr


def _ssm_prep(a_re, a_im, log_step, b_re, b_im):
    G, N = a_re.shape
    P = b_re.shape[-1]
    rep = lambda v: jnp.repeat(v, P, axis=0)
    flat = lambda v: jnp.swapaxes(v, 1, 2).reshape(G * P, N)
    ls = log_step.reshape(G, 1)
    small = jax.ShapeDtypeStruct((G, N), F32)
    big = jax.ShapeDtypeStruct((G * P, N), F32)
    return pl.pallas_call(
        _ssm_prep_body,
        out_shape=(small, small, small, small, big, big, big, big),
        name="ssm_discretise",
    )(a_re, a_im, ls, rep(a_re), rep(a_im), rep(ls), flat(b_re), flat(b_im))


def _ssm_block_operands(abr, abi, a2r, a2i, bbr, bbi, abbr, abbi, c_re, c_im):
    G, N = abr.shape
    P = SSM_GROUP
    gb = SSM_GROUPS_PER_BLOCK
    nb = G // gb
    eye = jnp.eye(gb, dtype=F32)

    def b_block(v):
        v = v.reshape(nb, gb, P, 1, N) * eye[None, :, None, :, None]
        return v.reshape(nb, gb * P, gb * N)

    def c_block(v):
        v = jnp.swapaxes(v.reshape(nb, gb, P, N), 2, 3)
        v = v.reshape(nb, gb, N, 1, P) * eye[None, :, None, :, None]
        return v.reshape(nb, gb * N, gb * P)

    bcur = jnp.concatenate([b_block(bbr), b_block(bbi)], axis=-1)
    bprev = jnp.concatenate([b_block(abbr), b_block(abbi)], axis=-1)
    bcat = jnp.concatenate([bcur, bprev], axis=1).astype(BF16)

    c8 = jnp.concatenate([c_block(c_re), -c_block(c_im)], axis=1)
    npair = SSM_BLOCKS_PER_READOUT
    eye2 = jnp.eye(npair, dtype=F32)
    cpair = c8.reshape(nb // npair, npair, 2 * gb * N, 1, gb * P) * eye2[None, :, None, :, None]
    cpair = cpair.reshape(nb // npair, npair * 2 * gb * N, npair * gb * P).astype(BF16)

    first = (lax.broadcasted_iota(jnp.int32, (1, 1, V7X_SUBLANES, 1), 2) < V7X_SUBLANES // 2)
    a1 = jnp.stack([abr, abi], axis=0).reshape(2, nb, 1, gb * N).swapaxes(0, 1)
    a2 = jnp.stack([a2r, a2i], axis=0).reshape(2, nb, 1, gb * N).swapaxes(0, 1)
    a_carry = jnp.where(first, a1, a2)
    return bcat, cpair, a_carry


def _gelu_tanh(x):
    c = math.sqrt(2.0 / math.pi)
    return x * (0.5 * (1.0 + jnp.tanh(c * (x + 0.044715 * (x * x * x)))))


def _ssm_body(x_ref, mod_ref, g_ref, d_ref, bcat_ref, cpair_ref, acar_ref, z_ref,
              hs_scr, hcat_scr, st_scr, y_scr, carry_scr, *, tt, nb):
    i = pl.program_id(0)
    B, _, D = x_ref.shape
    tm = tt * B
    sub = V7X_SUBLANES
    ns = SSM_BLOCK_STATES
    kin = 2 * V7X_LANES
    npair = SSM_BLOCKS_PER_READOUT
    cout = npair * V7X_LANES
    slab = lambda k: slice(k * V7X_LANES, (k + 1) * V7X_LANES)

    @pl.when(i == 0)
    def _():
        carry_scr[...] = jnp.zeros(carry_scr.shape, F32)

    for b in range(B):
        h = _norm_mod(x_ref[b], g_ref[...], mod_ref[b, 3:4, :], mod_ref[b, 4:5, :])
        for k in range(nb):
            hs_scr[k, pl.ds(b, tt, stride=B), :] = h[:, slab(k)]
    second = (lax.broadcasted_iota(jnp.int32, (tm, V7X_LANES), 0) & (sub // 2)) != 0
    for k in range(nb):
        hk = hs_scr[k]
        hcat_scr[:, k * kin:k * kin + V7X_LANES] = hk.astype(BF16)
        hcat_scr[:, k * kin + V7X_LANES:(k + 1) * kin] = jnp.where(
            second, pltpu.roll(hk, sub // 2, 0), 0.0).astype(BF16)

    first = lax.broadcasted_iota(jnp.int32, (sub, ns), 0) < sub // 2
    re_sl, im_sl = slice(0, ns), slice(ns, 2 * ns)

    for sb in range(nb):
        buf = sb % SSM_STATE_BUFFERS
        st_scr[buf] = jnp.dot(hcat_scr[:, sb * kin:(sb + 1) * kin], bcat_ref[sb],
                              preferred_element_type=F32)
        ar, ai = acar_ref[sb, 0], acar_ref[sb, 1]

        def step(r, carry, buf=buf, ar=ar, ai=ai):
            hr, hi = carry
            rows = pl.ds(pl.multiple_of(r * sub, sub), sub)
            sr = st_scr[buf, rows, re_sl] + (hr * ar - hi * ai)
            si = st_scr[buf, rows, im_sl] + (hr * ai + hi * ar)
            st_scr[buf, rows, re_sl] = sr
            st_scr[buf, rows, im_sl] = si
            return (jnp.where(first, pltpu.roll(sr, sub // 2, 0), sr),
                    jnp.where(first, pltpu.roll(si, sub // 2, 0), si))

        hr, hi = lax.fori_loop(0, tm // sub, step, (carry_scr[sb, 0], carry_scr[sb, 1]),
                               unroll=True)
        carry_scr[sb, 0] = hr
        carry_scr[sb, 1] = hi

        if sb % npair == npair - 1:
            pair = sb // npair
            acc = None
            for e in range(npair):
                states = st_scr[(sb - npair + 1 + e) % SSM_STATE_BUFFERS].astype(BF16)
                part = jnp.dot(states, cpair_ref[pair, e * 2 * ns:(e + 1) * 2 * ns, :],
                               preferred_element_type=F32)
                acc = part if acc is None else acc + part
            y_scr[:, pair * cout:(pair + 1) * cout] = acc

    for k in range(nb):
        hs_scr[k] = _gelu_tanh(y_scr[:, slab(k)] + d_ref[:, slab(k)] * hs_scr[k])
    for b in range(B):
        for k in range(nb):
            z_ref[b, :, slab(k)] = hs_scr[k, pl.ds(b, tt, stride=B), :].astype(BF16)


def _ssm(xv, mod, norm_g, d_skip, bcat, cpair, a_carry, *, layer, j, B, S, tt):
    D = norm_g.shape[-1]
    assert V7X_SUBLANES == 2 * B, "the scan packs 2 timesteps x B batches per 8-sublane row-group"
    nb = bcat.shape[0]
    assert nb == D // V7X_LANES
    tt = min(tt, S)
    tm = tt * B
    ns2 = 2 * SSM_BLOCK_STATES
    const = lambda *shape: _resident(shape, lambda i: (0,) * len(shape))
    vmem = 2 * tm * D * 4 + 2 * tm * D * 2 + 2 * tm * D * 4 + 2 * tm * D * 2 \
        + (SSM_STATE_BUFFERS + 2) * tm * ns2 * 4 + (bcat.size + cpair.size) * 2 + 2 * tm * D * 4 \
        + (6 << 20)
    return pl.pallas_call(
        functools.partial(_ssm_body, tt=tt, nb=nb),
        out_shape=jax.ShapeDtypeStruct((B, S, D), BF16),
        grid=(S // tt,),
        in_specs=[
            pl.BlockSpec((B, tt, D), lambda i: (0, i, 0)),
            pl.BlockSpec((None, B, N_SUBLAYERS * 3, D), lambda i: (layer, 0, 0, 0)),
            pl.BlockSpec((None, None, 1, D), lambda i: (layer, 1, 0, 0)),
            pl.BlockSpec((None, 1, D), lambda i: (j, 0, 0)),
            const(*bcat.shape),
            const(*cpair.shape),
            const(*a_carry.shape),
        ],
        out_specs=pl.BlockSpec((B, tt, D), lambda i: (0, i, 0)),
        scratch_shapes=[
            pltpu.VMEM((nb, tm, V7X_LANES), F32),
            pltpu.VMEM((tm, 2 * D), BF16),
            pltpu.VMEM((SSM_STATE_BUFFERS, tm, ns2), F32),
            pltpu.VMEM((tm, D), F32),
            pltpu.VMEM((nb, 2, V7X_SUBLANES, SSM_BLOCK_STATES), F32),
        ],
        compiler_params=_compiler_params(1, vmem),
        name=f"ssm_l{layer}",
    )(xv.reshape(B, S, D), mod, norm_g.reshape(norm_g.shape[0], N_SUBLAYERS, 1, D),
      d_skip.reshape(-1, 1, D), bcat, cpair, a_carry)


def _glu_body(x_ref, z_ref, wv_ref, wg_ref, mod_ref, o_ref):
    z = z_ref[...]
    val = jnp.dot(z, wv_ref[...], preferred_element_type=F32)
    gt = jnp.dot(z, wg_ref[...], preferred_element_type=F32)
    o_ref[...] = x_ref[...] + mod_ref[5:6, :] * (val * jax.nn.sigmoid(gt))


def _glu(xv, zv, w, mod, *, layer, j, B, S, tm, tn):
    D = w.shape[1]
    tm = min(tm, S)
    nt, nn = S // tm, D // tn
    vmem = 4 * tm * tn * 4 + 2 * tm * D * 2 + 4 * D * tn * 2 + 3 * tm * tn * 4 + (6 << 20)
    return pl.pallas_call(
        _glu_body,
        out_shape=jax.ShapeDtypeStruct((B * S, D), F32),
        grid=(B, nt, nn),
        in_specs=[
            pl.BlockSpec((tm, tn), lambda b, ti, jn: (b * nt + ti, jn)),
            pl.BlockSpec((tm, D), lambda b, ti, jn: (b * nt + ti, 0)),
            pl.BlockSpec((None, D, tn), lambda b, ti, jn: (j, 0, jn)),
            pl.BlockSpec((None, D, tn), lambda b, ti, jn: (j, 0, nn + jn)),
            pl.BlockSpec((None, None, N_SUBLAYERS * 3, tn), lambda b, ti, jn: (layer, b, 0, jn)),
        ],
        out_specs=pl.BlockSpec((tm, tn), lambda b, ti, jn: (b * nt + ti, jn)),
        compiler_params=_compiler_params(3, vmem),
        name=f"glu_l{layer}",
    )(xv, zv, w, w, mod)


def kernel(x, c, positions, norm_g, ada_w, ada_b, ffn_w_in, ffn_w_out, attn_w_in, attn_w_out,
           attn_q_norm, attn_k_norm, attn_lambda, attn_subln, ssm_a_re, ssm_a_im, ssm_log_step,
           ssm_b_re, ssm_b_im, ssm_c_re, ssm_c_im, ssm_d, ssm_w_glu):
    B, S, D = x.shape
    depth = norm_g.shape[0]

    mod = _adaln(c, ada_w, ada_b)
    attn_w_in = attn_w_in.astype(BF16)
    attn_w_out = attn_w_out.astype(BF16)
    ssm_w_glu = ssm_w_glu.astype(BF16)
    pos = positions.reshape(B * S, 1)

    ffn_order = [(i, which) for i in range(depth) for which in range(2)]
    ffn_w = (ffn_w_in[0, 0].astype(BF16), ffn_w_out[0, 0].astype(BF16))

    def ffn(xv, ffn_w, n):
        layer, which = ffn_order[n]
        nxt = (ffn_w_in, ffn_w_out) + ffn_order[n + 1] if n + 1 < len(ffn_order) else None
        return _ffn(xv, mod, norm_g, *ffn_w, nxt, layer=layer, sl=2 * which, B=B, S=S,
                    tm=1024, tf=512)

    xv = x.reshape(B * S, D)
    for i in range(depth):
        j = i // N_MIXERS
        xv, ffn_w = ffn(xv, ffn_w, 2 * i)
        if i % N_MIXERS == 0:
            lambda_init = 0.8 - 0.6 * math.exp(-0.3 * i)
            qkv = _qkv(xv, pos, mod, norm_g, attn_w_in, attn_q_norm, attn_k_norm,
                       layer=i, j=j, B=B, S=S, tm=512, tn=512)
            att = _attn(qkv, attn_lambda, attn_subln, j=j, B=B, S=S, D=D, tq=1024,
                        lambda_init=lambda_init)
            xv = _oproj(xv, att, attn_w_out, mod, layer=i, j=j, B=B, S=S, tm=512)
        else:
            disc = _ssm_prep(ssm_a_re[j], ssm_a_im[j], ssm_log_step[j], ssm_b_re[j], ssm_b_im[j])
            operands = _ssm_block_operands(*disc, ssm_c_re[j], ssm_c_im[j])
            z = _ssm(xv, mod, norm_g, ssm_d, *operands, layer=i, j=j, B=B, S=S, tt=64)
            xv = _glu(xv, z.reshape(B * S, D), ssm_w_glu, mod, layer=i, j=j, B=B, S=S,
                      tm=512, tn=1024)
        xv, ffn_w = ffn(xv, ffn_w, 2 * i + 1)
    return xv.reshape(B, S, D)
```

```python
import functools
import math

import numpy as np
import jax
import jax.numpy as jnp
from jax import lax
from jax.experimental import pallas as pl
from jax.experimental.pallas import tpu as pltpu

HEAD_DIM = 128
V_DIM = 2 * HEAD_DIM
ROT_DIM = HEAD_DIM // 4
ROPE_THETA = 500000.0
SSM_GROUP = 16
SSM_STATE = 64
N_SUBLAYERS = 3
N_MIXERS = 2
NORM_EPS = 1e-6
MACARON_WEIGHT = 0.5

V7X_LANES = 128
V7X_SUBLANES = 8
V7X_MXU_DIM = 256
V7X_VMEM_BYTES = 64 * 1024 * 1024
V7X_SCOPED_VMEM_CAP = 58 * 1024 * 1024

F32 = jnp.float32
BF16 = jnp.bfloat16
NEG_BIG = -1e30

SSM_GROUPS_PER_BLOCK = V7X_LANES // SSM_GROUP
SSM_BLOCK_STATES = SSM_GROUPS_PER_BLOCK * SSM_STATE
SSM_BLOCKS_PER_READOUT = V7X_MXU_DIM // V7X_LANES
SSM_STATE_BUFFERS = 2 * SSM_BLOCKS_PER_READOUT

QKV_ROW_CHUNKS = 2
FFN_FIRST_STEP_CHUNKS = 4
FFN_NORM_ROWS = 64


def _compiler_params(n_axes, vmem_bytes, flags=None):
    limit = min(int(vmem_bytes), V7X_SCOPED_VMEM_CAP)
    return pltpu.CompilerParams(dimension_semantics=("arbitrary",) * n_axes,
                                vmem_limit_bytes=limit, flags=flags)


def _resident(block_shape, index_map):
    return pl.BlockSpec(block_shape, index_map, pipeline_mode=pl.Buffered(1))


def _norm_mod(x, g, shift, scale):
    ms = jnp.mean(x * x, axis=-1, keepdims=True)
    y = x * lax.rsqrt(ms + NORM_EPS) * g
    return y * (1.0 + scale) + shift


def _adaln_body(c_ref, w_ref, b_ref, o_ref):
    c = c_ref[...]
    cond = (c * jax.nn.sigmoid(c)).astype(BF16)
    o_ref[...] = jnp.dot(cond, w_ref[...].astype(BF16), preferred_element_type=F32) + b_ref[...]


def _adaln(c, ada_w, ada_b):
    L, D, N = ada_w.shape
    B = c.shape[0]
    rows = V7X_SUBLANES * ((B + V7X_SUBLANES - 1) // V7X_SUBLANES)
    c_pad = jnp.pad(c, ((0, rows - B), (0, 0)))
    tn = D
    out = pl.pallas_call(
        _adaln_body,
        out_shape=jax.ShapeDtypeStruct((L, rows, N), F32),
        grid=(L, N // tn),
        in_specs=[
            pl.BlockSpec((rows, D), lambda l, j: (0, 0)),
            pl.BlockSpec((None, D, tn), lambda l, j: (l, 0, j)),
            pl.BlockSpec((None, 1, tn), lambda l, j: (l, 0, j)),
        ],
        out_specs=pl.BlockSpec((None, rows, tn), lambda l, j: (l, 0, j)),
        compiler_params=_compiler_params(2, 2 * D * tn * 4 + D * tn * 2 + (8 << 20)),
        name="adaln",
    )(c_pad, ada_w, ada_b.reshape(L, 1, N))
    return out[:, :B].reshape(L, B, N_SUBLAYERS * 3, D)


def _ffn_body(*refs, sl, cast_next):
    if cast_next:
        (x_ref, mod_ref, g_ref, wa_ref, wb_ref, wo_ref, nwi_ref, nwo_ref,
         o_ref, nwi_out_ref, nwo_out_ref, h_scr) = refs
        nwi_out_ref[...] = nwi_ref[...].astype(BF16)
        nwo_out_ref[...] = nwo_ref[...].astype(BF16)
    else:
        x_ref, mod_ref, g_ref, wa_ref, wb_ref, wo_ref, o_ref, h_scr = refs
    j = pl.program_id(2)
    nj = pl.num_programs(2)
    tm = x_ref.shape[0]

    def swiglu(h):
        a = jnp.dot(h, wa_ref[...], preferred_element_type=F32)
        b = jnp.dot(h, wb_ref[...], preferred_element_type=F32)
        act = (a * jax.nn.sigmoid(a) * b).astype(BF16)
        return jnp.dot(act, wo_ref[...], preferred_element_type=F32)

    @pl.when(j == 0)
    def _():
        shift, scale = mod_ref[3 * sl:3 * sl + 1, :], mod_ref[3 * sl + 1:3 * sl + 2, :]
        rc = tm // FFN_FIRST_STEP_CHUNKS
        nr = min(FFN_NORM_ROWS, rc)
        for r in range(FFN_FIRST_STEP_CHUNKS):
            for lo in range(r * rc, (r + 1) * rc, nr):
                h_scr[lo:lo + nr, :] = _norm_mod(x_ref[lo:lo + nr, :], g_ref[...], shift,
                                                 scale).astype(BF16)
            o_ref[r * rc:(r + 1) * rc, :] = swiglu(h_scr[r * rc:(r + 1) * rc, :])

    @pl.when((j > 0) & (j < nj - 1))
    def _():
        o_ref[...] += swiglu(h_scr[...])

    @pl.when(j == nj - 1)
    def _():
        gate = mod_ref[3 * sl + 2:3 * sl + 3, :]
        o_ref[...] = x_ref[...] + (MACARON_WEIGHT * gate) * (o_ref[...] + swiglu(h_scr[...]))


def _ffn(xv, mod, norm_g, w_in, w_out, next_w, *, layer, sl, B, S, tm, tf):
    D = norm_g.shape[-1]
    F = w_out.shape[0]
    tm = min(tm, S)
    nt, nj = S // tm, F // tf
    assert nj >= 2, "first and last F steps are distinct code paths"
    rows = lambda b, ti, j: (b * nt + ti, 0)
    in_specs = [
        pl.BlockSpec((tm, D), rows),
        pl.BlockSpec((None, None, N_SUBLAYERS * 3, D), lambda b, ti, j: (layer, b, 0, 0)),
        pl.BlockSpec((None, None, 1, D), lambda b, ti, j: (layer, sl, 0, 0)),
        pl.BlockSpec((D, tf), lambda b, ti, j: (0, j)),
        pl.BlockSpec((D, tf), lambda b, ti, j: (0, nj + j)),
        pl.BlockSpec((tf, D), lambda b, ti, j: (j, 0)),
    ]
    operands = [xv, mod, norm_g.reshape(norm_g.shape[0], N_SUBLAYERS, 1, D), w_in, w_in, w_out]
    out_specs = [pl.BlockSpec((tm, D), rows)]
    out_shape = [jax.ShapeDtypeStruct((B * S, D), F32)]
    if next_w is not None:
        nwi, nwo, l2, w2 = next_w
        steps = B * nt * nj
        assert D % (B * nt) == 0 and (2 * F) % nj == 0 and F % steps == 0
        rin, cin, rout = D // (B * nt), 2 * F // nj, F // steps
        in_specs += [
            pl.BlockSpec((None, None, rin, cin), lambda b, ti, j: (l2, w2, b * nt + ti, j)),
            pl.BlockSpec((None, None, rout, D), lambda b, ti, j: (l2, w2, (b * nt + ti) * nj + j, 0)),
        ]
        operands += [nwi, nwo]
        out_specs += [
            pl.BlockSpec((rin, cin), lambda b, ti, j: (b * nt + ti, j)),
            pl.BlockSpec((rout, D), lambda b, ti, j: ((b * nt + ti) * nj + j, 0)),
        ]
        out_shape += [jax.ShapeDtypeStruct((D, 2 * F), BF16), jax.ShapeDtypeStruct((F, D), BF16)]
    vmem = 4 * tm * D * 4 + tm * D * 2 + 2 * 3 * D * tf * 2 + 4 * tm * tf * 4 + (6 << 20)
    out = pl.pallas_call(
        functools.partial(_ffn_body, sl=sl, cast_next=next_w is not None),
        out_shape=out_shape,
        grid=(B, nt, nj),
        in_specs=in_specs,
        out_specs=out_specs,
        scratch_shapes=[pltpu.VMEM((tm, D), BF16)],
        compiler_params=_compiler_params(3, vmem),
        name=f"ffn_l{layer}_s{sl}",
    )(*operands)
    return out[0], tuple(out[1:])


def _qkv_body(x_ref, pos_ref, invf_ref, mod_ref, g_ref, w_ref, gain_ref, o_ref, *, tn, q_mult):
    tm, D = x_ref.shape
    half = ROT_DIM // 2
    gains = (gain_ref[0] * q_mult, gain_ref[1])
    rc = tm // QKV_ROW_CHUNKS

    for rows in (slice(r * rc, (r + 1) * rc) for r in range(QKV_ROW_CHUNKS)):
        h = _norm_mod(x_ref[rows, :], g_ref[...], mod_ref[3:4, :], mod_ref[4:5, :]).astype(BF16)
        ang = pos_ref[rows, :].astype(F32) * invf_ref[...]
        c, s = jnp.cos(ang), jnp.sin(ang)
        lane = lax.broadcasted_iota(jnp.int32, ang.shape, 1)
        cos = jnp.where(lane < ROT_DIM, c, 1.0)
        sin_a = jnp.where(lane < half, -s, 0.0)
        sin_b = jnp.where((lane >= half) & (lane < ROT_DIM), s, 0.0)

        for jn in range(w_ref.shape[-1] // tn):
            cols = slice(jn * tn, (jn + 1) * tn)
            acc = jnp.dot(h, w_ref[:, cols], preferred_element_type=F32)
            which = (jn * tn) // D
            if which == 2:
                o_ref[rows, cols] = acc.astype(BF16)
                continue
            for ch in range(tn // HEAD_DIM):
                t = acc[:, ch * HEAD_DIM:(ch + 1) * HEAD_DIM]
                ms = jnp.mean(t * t, axis=-1, keepdims=True)
                y = t * lax.rsqrt(ms + NORM_EPS) * gains[which]
                r = (y * cos + pltpu.roll(y, HEAD_DIM - half, 1) * sin_a
                     + pltpu.roll(y, half, 1) * sin_b)
                lo = jn * tn + ch * HEAD_DIM
                o_ref[rows, lo:lo + HEAD_DIM] = r.astype(BF16)


def _qkv(xv, pos, mod, norm_g, w, q_gain, k_gain, *, layer, j, B, S, tm, tn):
    D = norm_g.shape[-1]
    N = w.shape[-1]
    tm = min(tm, S)
    nt = S // tm
    half = ROT_DIM // 2
    inv_freq = ROPE_THETA ** (-np.arange(0, ROT_DIM, 2, dtype=np.float64) / ROT_DIM)
    invf = jnp.asarray(np.tile(inv_freq, HEAD_DIM // half)[None, :], dtype=F32)
    gains = jnp.stack([q_gain, k_gain], axis=1).reshape(-1, 2, 1, HEAD_DIM)
    q_mult = HEAD_DIM ** -0.5 * math.log2(math.e)
    rows = lambda b, ti: (b * nt + ti, 0)
    vmem = 2 * tm * D * 4 + tm * D * 2 + D * N * 2 + 2 * tm * N * 2 + 4 * tm * tn * 4 \
        + 8 * tm * V7X_LANES * 4 + (6 << 20)
    return pl.pallas_call(
        functools.partial(_qkv_body, tn=tn, q_mult=q_mult),
        out_shape=jax.ShapeDtypeStruct((B * S, N), BF16),
        grid=(B, nt),
        in_specs=[
            pl.BlockSpec((tm, D), rows),
            pl.BlockSpec((tm, 1), rows),
            pl.BlockSpec((1, HEAD_DIM), lambda b, ti: (0, 0)),
            pl.BlockSpec((None, None, N_SUBLAYERS * 3, D), lambda b, ti: (layer, b, 0, 0)),
            pl.BlockSpec((None, None, 1, D), lambda b, ti: (layer, 1, 0, 0)),
            _resident((None, D, N), lambda b, ti: (j, 0, 0)),
            pl.BlockSpec((None, 2, 1, HEAD_DIM), lambda b, ti: (j, 0, 0, 0)),
        ],
        out_specs=pl.BlockSpec((tm, N), rows),
        compiler_params=_compiler_params(2, vmem),
        name=f"qkv_l{layer}",
    )(xv, pos, invf, mod, norm_g.reshape(norm_g.shape[0], N_SUBLAYERS, 1, D), w, gains)


def _attn_body(q_ref, k_ref, v_ref, lam_ref, sub_ref, o_ref, m_scr, l_scr, acc_scr,
               sa_scr, sb_scr, *, tq, lambda_init):
    qi = pl.program_id(2)
    tk = tq // 2
    m_scr[...] = jnp.full(m_scr.shape, NEG_BIG, F32)
    l_scr[...] = jnp.zeros(l_scr.shape, F32)
    acc_scr[...] = jnp.zeros(acc_scr.shape, F32)
    q = q_ref[...]
    key = lax.broadcasted_iota(jnp.int32, (tk, tq), 0)
    qry = lax.broadcasted_iota(jnp.int32, (tk, tq), 1)
    causal = key <= qry

    def scores(kb, s_scr, lo=0):
        kblk = k_ref[pl.ds(pl.multiple_of(kb * tk, tk), tk), :]
        for c in range(2):
            qc = q[lo:, c * HEAD_DIM:(c + 1) * HEAD_DIM]
            kc = kblk[:, c * HEAD_DIM:(c + 1) * HEAD_DIM]
            s_scr[c, :, lo:] = lax.dot_general(kc, qc, (((1,), (1,)), ((), ())),
                                               preferred_element_type=F32)

    def consume(kb, s_scr, lo=0, masked=False):
        vblk = v_ref[pl.ds(pl.multiple_of(kb * tk, tk), tk), :]
        for c in range(2):
            s = s_scr[c, :, lo:]
            if masked:
                s = jnp.where(causal[:, :tq - lo], s, NEG_BIG)
            m_prev = m_scr[c, :, lo:]
            m_new = jnp.maximum(m_prev, jnp.max(s, axis=0, keepdims=True))
            p = jnp.exp2(s - m_new)
            alpha = jnp.exp2(m_prev - m_new)
            l_scr[c, :, lo:] = alpha * l_scr[c, :, lo:] + jnp.sum(p, axis=0, keepdims=True)
            pv = lax.dot_general(vblk, p.astype(BF16), (((0,), (0,)), ((), ())),
                                 preferred_element_type=F32)
            acc_scr[c, :, lo:] = alpha * acc_scr[c, :, lo:] + pv
            m_scr[c, :, lo:] = m_new

    scores(0, sa_scr)

    def pair(p, carry):
        scores(2 * p + 1, sb_scr)
        consume(2 * p, sa_scr)
        scores(2 * p + 2, sa_scr)
        consume(2 * p + 1, sb_scr)
        return carry

    lax.fori_loop(0, qi, pair, 0)
    scores(2 * qi + 1, sb_scr, lo=tk)
    consume(2 * qi, sa_scr, masked=True)
    consume(2 * qi + 1, sb_scr, lo=tk, masked=True)

    lv = lam_ref[...]
    lam = (jnp.exp(jnp.sum(lv[0:1] * lv[1:2], axis=-1, keepdims=True))
           - jnp.exp(jnp.sum(lv[2:3] * lv[3:4], axis=-1, keepdims=True)) + lambda_init)
    o = acc_scr[0] / l_scr[0] - lam * (acc_scr[1] / l_scr[1])
    ms = jnp.mean(o * o, axis=0, keepdims=True)
    o = (o * lax.rsqrt(ms + NORM_EPS)).T
    o_ref[...] = ((o * sub_ref[...]) * (1.0 - lambda_init)).astype(BF16)


def _attn(qkv, lam, subln, *, j, B, S, D, tq, lambda_init):
    H = D // V_DIM
    tq = min(tq, S)
    nq = S // tq
    vmem = 2 * 2 * S * V_DIM * 2 + 4 * tq * V_DIM * 2 + 2 * tq * V_DIM * 4 \
        + 4 * tq * V7X_SUBLANES * 4 + 6 * tq * tq * 4 + (6 << 20)
    return pl.pallas_call(
        functools.partial(_attn_body, tq=tq, lambda_init=lambda_init),
        out_shape=jax.ShapeDtypeStruct((B * S, D), BF16),
        grid=(B, H, nq),
        in_specs=[
            pl.BlockSpec((tq, V_DIM), lambda b, h, qi: (b * nq + qi, h)),
            pl.BlockSpec((S, V_DIM), lambda b, h, qi: (b, H + h)),
            pl.BlockSpec((S, V_DIM), lambda b, h, qi: (b, 2 * H + h)),
            pl.BlockSpec((None, 4, HEAD_DIM), lambda b, h, qi: (j, 0, 0)),
            pl.BlockSpec((None, 1, V_DIM), lambda b, h, qi: (j, 0, 0)),
        ],
        out_specs=pl.BlockSpec((tq, V_DIM), lambda b, h, qi: (b * nq + qi, h)),
        scratch_shapes=[pltpu.VMEM((2, 1, tq), F32), pltpu.VMEM((2, 1, tq), F32),
                        pltpu.VMEM((2, V_DIM, tq), F32),
                        pltpu.VMEM((2, tq // 2, tq), F32), pltpu.VMEM((2, tq // 2, tq), F32)],
        compiler_params=_compiler_params(3, vmem),
        name="diff_attn",
    )(qkv, qkv, qkv, lam, subln.reshape(-1, 1, V_DIM))


def _oproj_body(x_ref, a_ref, w_ref, mod_ref, o_ref):
    m = jnp.dot(a_ref[...], w_ref[...], preferred_element_type=F32)
    o_ref[...] = x_ref[...] + mod_ref[5:6, :] * m


def _oproj(xv, a, w, mod, *, layer, j, B, S, tm):
    D = xv.shape[-1]
    tm = min(tm, S)
    nt = S // tm
    rows = lambda b, ti: (b * nt + ti, 0)
    vmem = 4 * tm * D * 4 + 2 * tm * D * 2 + D * D * 2 + tm * D * 4 + (6 << 20)
    return pl.pallas_call(
        _oproj_body,
        out_shape=jax.ShapeDtypeStruct((B * S, D), F32),
        grid=(B, nt),
        in_specs=[
            pl.BlockSpec((tm, D), rows),
            pl.BlockSpec((tm, D), rows),
            _resident((None, D, D), lambda b, ti: (j, 0, 0)),
            pl.BlockSpec((None, None, N_SUBLAYERS * 3, D), lambda b, ti: (layer, b, 0, 0)),
        ],
        out_specs=pl.BlockSpec((tm, D), rows),
        compiler_params=_compiler_params(2, vmem),
        name=f"attn_out_l{layer}",
    )(xv, a, w, mod)


def _ssm_prep_body(ar_ref, ai_ref, ls_ref, arr_ref, air_ref, lsr_ref, br_ref, bi_ref,
                   abr_ref, abi_ref, a2r_ref, a2i_ref, bbr_ref, bbi_ref, abbr_ref, abbi_ref):
    def abar(ar, ai, ls):
        dt = jnp.exp(ls)
        mag = jnp.exp(dt * ar)
        return mag * jnp.cos(dt * ai), mag * jnp.sin(dt * ai)

    re, im = abar(ar_ref[...], ai_ref[...], ls_ref[...])
    abr_ref[...] = re
    abi_ref[...] = im
    a2r_ref[...] = re * re - im * im
    a2i_ref[...] = 2.0 * (re * im)

    ar, ai = arr_ref[...], air_ref[...]
    re, im = abar(ar, ai, lsr_ref[...])
    den = ar * ar + ai * ai
    num_re = re - 1.0
    coef_re = (num_re * ar + im * ai) / den
    coef_im = (im * ar - num_re * ai) / den
    br, bi = br_ref[...], bi_ref[...]
    bbr = coef_re * br - coef_im * bi
    bbi = coef_re * bi + coef_im * br
    bbr_ref[...] = bbr
    bbi_ref[...] = bbi
    abbr_ref[...] = re * bbr - im * bbi
    abbi_ref[...] = re * bbi + im * bbr


def _ssm_prep(a_re, a_im, log_step, b_re, b_im):
    G, N = a_re.shape
    P = b_re.shape[-1]
    rep = lambda v: jnp.repeat(v, P, axis=0)
    flat = lambda v: jnp.swapaxes(v, 1, 2).reshape(G * P, N)
    ls = log_step.reshape(G, 1)
    small = jax.ShapeDtypeStruct((G, N), F32)
    big = jax.ShapeDtypeStruct((G * P, N), F32)
    return pl.pallas_call(
        _ssm_prep_body,
        out_shape=(small, small, small, small, big, big, big, big),
        name="ssm_discretise",
    )(a_re, a_im, ls, rep(a_re), rep(a_im), rep(ls), flat(b_re), flat(b_im))


def _ssm_block_operands(abr, abi, a2r, a2i, bbr, bbi, abbr, abbi, c_re, c_im):
    G, N = abr.shape
    P = SSM_GROUP
    gb = SSM_GROUPS_PER_BLOCK
    nb = G // gb
    eye = jnp.eye(gb, dtype=F32)

    def b_block(v):
        v = v.reshape(nb, gb, P, 1, N) * eye[None, :, None, :, None]
        return v.reshape(nb, gb * P, gb * N)

    def c_block(v):
        v = jnp.swapaxes(v.reshape(nb, gb, P, N), 2, 3)
        v = v.reshape(nb, gb, N, 1, P) * eye[None, :, None, :, None]
        return v.reshape(nb, gb * N, gb * P)

    bcur = jnp.concatenate([b_block(bbr), b_block(bbi)], axis=-1)
    bprev = jnp.concatenate([b_block(abbr), b_block(abbi)], axis=-1)
    bcat = jnp.concatenate([bcur, bprev], axis=1).astype(BF16)

    c8 = jnp.concatenate([c_block(c_re), -c_block(c_im)], axis=1)
    npair = SSM_BLOCKS_PER_READOUT
    eye2 = jnp.eye(npair, dtype=F32)
    cpair = c8.reshape(nb // npair, npair, 2 * gb * N, 1, gb * P) * eye2[None, :, None, :, None]
    cpair = cpair.reshape(nb // npair, npair * 2 * gb * N, npair * gb * P).astype(BF16)

    first = (lax.broadcasted_iota(jnp.int32, (1, 1, V7X_SUBLANES, 1), 2) < V7X_SUBLANES // 2)
    a1 = jnp.stack([abr, abi], axis=0).reshape(2, nb, 1, gb * N).swapaxes(0, 1)
    a2 = jnp.stack([a2r, a2i], axis=0).reshape(2, nb, 1, gb * N).swapaxes(0, 1)
    a_carry = jnp.where(first, a1, a2)
    return bcat, cpair, a_carry


def _gelu_tanh(x):
    c = math.sqrt(2.0 / math.pi)
    return x * (0.5 * (1.0 + jnp.tanh(c * (x + 0.044715 * (x * x * x)))))


def _ssm_body(x_ref, mod_ref, g_ref, d_ref, bcat_ref, cpair_ref, acar_ref, z_ref,
              hs_scr, hcat_scr, st_scr, y_scr, carry_scr, *, tt, nb):
    i = pl.program_id(0)
    B, _, D = x_ref.shape
    tm = tt * B
    sub = V7X_SUBLANES
    ns = SSM_BLOCK_STATES
    kin = 2 * V7X_LANES
    npair = SSM_BLOCKS_PER_READOUT
    cout = npair * V7X_LANES
    slab = lambda k: slice(k * V7X_LANES, (k + 1) * V7X_LANES)

    @pl.when(i == 0)
    def _():
        carry_scr[...] = jnp.zeros(carry_scr.shape, F32)

    for b in range(B):
        h = _norm_mod(x_ref[b], g_ref[...], mod_ref[b, 3:4, :], mod_ref[b, 4:5, :])
        for k in range(nb):
            hs_scr[k, pl.ds(b, tt, stride=B), :] = h[:, slab(k)]
    second = (lax.broadcasted_iota(jnp.int32, (tm, V7X_LANES), 0) & (sub // 2)) != 0
    for k in range(nb):
        hk = hs_scr[k]
        hcat_scr[:, k * kin:k * kin + V7X_LANES] = hk.astype(BF16)
        hcat_scr[:, k * kin + V7X_LANES:(k + 1) * kin] = jnp.where(
            second, pltpu.roll(hk, sub // 2, 0), 0.0).astype(BF16)

    first = lax.broadcasted_iota(jnp.int32, (sub, ns), 0) < sub // 2
    re_sl, im_sl = slice(0, ns), slice(ns, 2 * ns)

    for sb in range(nb):
        buf = sb % SSM_STATE_BUFFERS
        st_scr[buf] = jnp.dot(hcat_scr[:, sb * kin:(sb + 1) * kin], bcat_ref[sb],
                              preferred_element_type=F32)
        ar, ai = acar_ref[sb, 0], acar_ref[sb, 1]

        def step(r, carry, buf=buf, ar=ar, ai=ai):
            hr, hi = carry
            rows = pl.ds(pl.multiple_of(r * sub, sub), sub)
            sr = st_scr[buf, rows, re_sl] + (hr * ar - hi * ai)
            si = st_scr[buf, rows, im_sl] + (hr * ai + hi * ar)
            st_scr[buf, rows, re_sl] = sr
            st_scr[buf, rows, im_sl] = si
            return (jnp.where(first, pltpu.roll(sr, sub // 2, 0), sr),
                    jnp.where(first, pltpu.roll(si, sub // 2, 0), si))

        hr, hi = lax.fori_loop(0, tm // sub, step, (carry_scr[sb, 0], carry_scr[sb, 1]),
                               unroll=True)
        carry_scr[sb, 0] = hr
        carry_scr[sb, 1] = hi

        if sb % npair == npair - 1:
            pair = sb // npair
            acc = None
            for e in range(npair):
                states = st_scr[(sb - npair + 1 + e) % SSM_STATE_BUFFERS].astype(BF16)
                part = jnp.dot(states, cpair_ref[pair, e * 2 * ns:(e + 1) * 2 * ns, :],
                               preferred_element_type=F32)
                acc = part if acc is None else acc + part
            y_scr[:, pair * cout:(pair + 1) * cout] = acc

    for k in range(nb):
        hs_scr[k] = _gelu_tanh(y_scr[:, slab(k)] + d_ref[:, slab(k)] * hs_scr[k])
    for b in range(B):
        for k in range(nb):
            z_ref[b, :, slab(k)] = hs_scr[k, pl.ds(b, tt, stride=B), :].astype(BF16)


def _ssm(xv, mod, norm_g, d_skip, bcat, cpair, a_carry, *, layer, j, B, S, tt):
    D = norm_g.shape[-1]
    assert V7X_SUBLANES == 2 * B, "the scan packs 2 timesteps x B batches per 8-sublane row-group"
    nb = bcat.shape[0]
    assert nb == D // V7X_LANES
    tt = min(tt, S)
    tm = tt * B
    ns2 = 2 * SSM_BLOCK_STATES
    const = lambda *shape: _resident(shape, lambda i: (0,) * len(shape))
    vmem = 2 * tm * D * 4 + 2 * tm * D * 2 + 2 * tm * D * 4 + 2 * tm * D * 2 \
        + (SSM_STATE_BUFFERS + 2) * tm * ns2 * 4 + (bcat.size + cpair.size) * 2 + 2 * tm * D * 4 \
        + (6 << 20)
    return pl.pallas_call(
        functools.partial(_ssm_body, tt=tt, nb=nb),
        out_shape=jax.ShapeDtypeStruct((B, S, D), BF16),
        grid=(S // tt,),
        in_specs=[
            pl.BlockSpec((B, tt, D), lambda i: (0, i, 0)),
            pl.BlockSpec((None, B, N_SUBLAYERS * 3, D), lambda i: (layer, 0, 0, 0)),
            pl.BlockSpec((None, None, 1, D), lambda i: (layer, 1, 0, 0)),
            pl.BlockSpec((None, 1, D), lambda i: (j, 0, 0)),
            const(*bcat.shape),
            const(*cpair.shape),
            const(*a_carry.shape),
        ],
        out_specs=pl.BlockSpec((B, tt, D), lambda i: (0, i, 0)),
        scratch_shapes=[
            pltpu.VMEM((nb, tm, V7X_LANES), F32),
            pltpu.VMEM((tm, 2 * D), BF16),
            pltpu.VMEM((SSM_STATE_BUFFERS, tm, ns2), F32),
            pltpu.VMEM((tm, D), F32),
            pltpu.VMEM((nb, 2, V7X_SUBLANES, SSM_BLOCK_STATES), F32),
        ],
        compiler_params=_compiler_params(1, vmem),
        name=f"ssm_l{layer}",
    )(xv.reshape(B, S, D), mod, norm_g.reshape(norm_g.shape[0], N_SUBLAYERS, 1, D),
      d_skip.reshape(-1, 1, D), bcat, cpair, a_carry)


def _glu_body(x_ref, z_ref, wv_ref, wg_ref, mod_ref, o_ref):
    z = z_ref[...]
    val = jnp.dot(z, wv_ref[...], preferred_element_type=F32)
    gt = jnp.dot(z, wg_ref[...], preferred_element_type=F32)
    o_ref[...] = x_ref[...] + mod_ref[5:6, :] * (val * jax.nn.sigmoid(gt))


def _glu(xv, zv, w, mod, *, layer, j, B, S, tm, tn):
    D = w.shape[1]
    tm = min(tm, S)
    nt, nn = S // tm, D // tn
    vmem = 4 * tm * tn * 4 + 2 * tm * D * 2 + 4 * D * tn * 2 + 3 * tm * tn * 4 + (6 << 20)
    return pl.pallas_call(
        _glu_body,
        out_shape=jax.ShapeDtypeStruct((B * S, D), F32),
        grid=(nn, B, nt),
        in_specs=[
            pl.BlockSpec((tm, tn), lambda jn, b, ti: (b * nt + ti, jn)),
            pl.BlockSpec((tm, D), lambda jn, b, ti: (b * nt + ti, 0)),
            pl.BlockSpec((None, D, tn), lambda jn, b, ti: (j, 0, jn)),
            pl.BlockSpec((None, D, tn), lambda jn, b, ti: (j, 0, nn + jn)),
            pl.BlockSpec((None, None, N_SUBLAYERS * 3, tn), lambda jn, b, ti: (layer, b, 0, jn)),
        ],
        out_specs=pl.BlockSpec((tm, tn), lambda jn, b, ti: (b * nt + ti, jn)),
        compiler_params=_compiler_params(3, vmem),
        name=f"glu_l{layer}",
    )(xv, zv, w, w, mod)


def kernel(x, c, positions, norm_g, ada_w, ada_b, ffn_w_in, ffn_w_out, attn_w_in, attn_w_out,
           attn_q_norm, attn_k_norm, attn_lambda, attn_subln, ssm_a_re, ssm_a_im, ssm_log_step,
           ssm_b_re, ssm_b_im, ssm_c_re, ssm_c_im, ssm_d, ssm_w_glu):
    B, S, D = x.shape
    depth = norm_g.shape[0]

    mod = _adaln(c, ada_w, ada_b)
    attn_w_in = attn_w_in.astype(BF16)
    attn_w_out = attn_w_out.astype(BF16)
    ssm_w_glu = ssm_w_glu.astype(BF16)
    pos = positions.reshape(B * S, 1)

    ffn_order = [(i, which) for i in range(depth) for which in range(2)]
    ffn_w = (ffn_w_in[0, 0].astype(BF16), ffn_w_out[0, 0].astype(BF16))

    def ffn(xv, ffn_w, n):
        layer, which = ffn_order[n]
        nxt = (ffn_w_in, ffn_w_out) + ffn_order[n + 1] if n + 1 < len(ffn_order) else None
        return _ffn(xv, mod, norm_g, *ffn_w, nxt, layer=layer, sl=2 * which, B=B, S=S,
                    tm=1024, tf=512)

    xv = x.reshape(B * S, D)
    for i in range(depth):
        j = i // N_MIXERS
        xv, ffn_w = ffn(xv, ffn_w, 2 * i)
        if i % N_MIXERS == 0:
            lambda_init = 0.8 - 0.6 * math.exp(-0.3 * i)
            qkv = _qkv(xv, pos, mod, norm_g, attn_w_in, attn_q_norm, attn_k_norm,
                       layer=i, j=j, B=B, S=S, tm=512, tn=512)
            att = _attn(qkv, attn_lambda, attn_subln, j=j, B=B, S=S, D=D, tq=1024,
                        lambda_init=lambda_init)
            xv = _oproj(xv, att, attn_w_out, mod, layer=i, j=j, B=B, S=S, tm=512)
        else:
            disc = _ssm_prep(ssm_a_re[j], ssm_a_im[j], ssm_log_step[j], ssm_b_re[j], ssm_b_im[j])
            operands = _ssm_block_operands(*disc, ssm_c_re[j], ssm_c_im[j])
            z = _ssm(xv, mod, norm_g, ssm_d, *operands, layer=i, j=j, B=B, S=S, tt=64)
            xv = _glu(xv, z.reshape(B * S, D), ssm_w_glu, mod, layer=i, j=j, B=B, S=S,
                      tm=512, tn=1024)
        xv, ffn_w = ffn(xv, ffn_w, 2 * i + 1)
    return xv.reshape(B, S, D)
```

```python
import functools
import math

import numpy as np
import jax
import jax.numpy as jnp
from jax import lax
from jax.experimental import pallas as pl
from jax.experimental.pallas import tpu as pltpu

HEAD_DIM = 128
V_DIM = 2 * HEAD_DIM
ROT_DIM = HEAD_DIM // 4
ROPE_THETA = 500000.0
SSM_GROUP = 16
SSM_STATE = 64
N_SUBLAYERS = 3
N_MIXERS = 2
NORM_EPS = 1e-6
MACARON_WEIGHT = 0.5

V7X_LANES = 128
V7X_SUBLANES = 8
V7X_MXU_DIM = 256
V7X_SCOPED_VMEM_CAP = 58 * 1024 * 1024

F32 = jnp.float32
BF16 = jnp.bfloat16
NEG_BIG = -1e30

SSM_GROUPS_PER_BLOCK = V7X_LANES // SSM_GROUP
SSM_BLOCK_STATES = SSM_GROUPS_PER_BLOCK * SSM_STATE
SSM_BLOCKS_PER_READOUT = V7X_MXU_DIM // V7X_LANES
SSM_STATE_BUFFERS = 2 * SSM_BLOCKS_PER_READOUT

FFN_ROWS, FFN_F_TILE = 1024, 512
QKV_ROWS, QKV_COL_CHUNK = 512, 512
ATTN_QUERY_ROWS = 1024
OPROJ_ROWS = 512
SSM_TIMESTEPS = 64
GLU_ROWS, GLU_COL_TILE = 1024, 1024

QKV_ROW_CHUNKS = 2
FFN_FIRST_STEP_CHUNKS = 4
FFN_NORM_ROWS = 64


def _compiler_params(n_axes, vmem_bytes):
    limit = min(int(vmem_bytes), V7X_SCOPED_VMEM_CAP)
    return pltpu.CompilerParams(dimension_semantics=("arbitrary",) * n_axes,
                                vmem_limit_bytes=limit)


def _resident(block_shape, index_map):
    return pl.BlockSpec(block_shape, index_map, pipeline_mode=pl.Buffered(1))


def _norm_mod(x, g, shift, scale):
    ms = jnp.mean(x * x, axis=-1, keepdims=True)
    y = x * lax.rsqrt(ms + NORM_EPS) * g
    return y * (1.0 + scale) + shift


def _adaln_body(c_ref, w_ref, b_ref, o_ref):
    c = c_ref[...]
    cond = (c * jax.nn.sigmoid(c)).astype(BF16)
    o_ref[...] = jnp.dot(cond, w_ref[...].astype(BF16), preferred_element_type=F32) + b_ref[...]


def _adaln(c, ada_w, ada_b):
    L, D, N = ada_w.shape
    B = c.shape[0]
    rows = V7X_SUBLANES * ((B + V7X_SUBLANES - 1) // V7X_SUBLANES)
    c_pad = jnp.pad(c, ((0, rows - B), (0, 0)))
    tn = D
    out = pl.pallas_call(
        _adaln_body,
        out_shape=jax.ShapeDtypeStruct((L, rows, N), F32),
        grid=(L, N // tn),
        in_specs=[
            pl.BlockSpec((rows, D), lambda l, j: (0, 0)),
            pl.BlockSpec((None, D, tn), lambda l, j: (l, 0, j)),
            pl.BlockSpec((None, 1, tn), lambda l, j: (l, 0, j)),
        ],
        out_specs=pl.BlockSpec((None, rows, tn), lambda l, j: (l, 0, j)),
        compiler_params=_compiler_params(2, 2 * D * tn * 4 + D * tn * 2 + (8 << 20)),
        name="adaln",
    )(c_pad, ada_w, ada_b.reshape(L, 1, N))
    return out[:, :B].reshape(L, B, N_SUBLAYERS * 3, D)


def _ffn_body(*refs, sl, cast_next):
    if cast_next:
        (x_ref, mod_ref, g_ref, wa_ref, wb_ref, wo_ref, nwi_ref, nwo_ref,
         o_ref, nwi_out_ref, nwo_out_ref, h_scr) = refs
        nwi_out_ref[...] = nwi_ref[...].astype(BF16)
        nwo_out_ref[...] = nwo_ref[...].astype(BF16)
    else:
        x_ref, mod_ref, g_ref, wa_ref, wb_ref, wo_ref, o_ref, h_scr = refs
    j = pl.program_id(2)
    nj = pl.num_programs(2)
    tm = x_ref.shape[0]

    def swiglu(h):
        a = jnp.dot(h, wa_ref[...], preferred_element_type=F32)
        b = jnp.dot(h, wb_ref[...], preferred_element_type=F32)
        act = (a * jax.nn.sigmoid(a) * b).astype(BF16)
        return jnp.dot(act, wo_ref[...], preferred_element_type=F32)

    @pl.when(j == 0)
    def _():
        shift, scale = mod_ref[3 * sl:3 * sl + 1, :], mod_ref[3 * sl + 1:3 * sl + 2, :]
        rc = tm // FFN_FIRST_STEP_CHUNKS
        nr = min(FFN_NORM_ROWS, rc)
        for r in range(FFN_FIRST_STEP_CHUNKS):
            for lo in range(r * rc, (r + 1) * rc, nr):
                h_scr[lo:lo + nr, :] = _norm_mod(x_ref[lo:lo + nr, :], g_ref[...], shift,
                                                 scale).astype(BF16)
            o_ref[r * rc:(r + 1) * rc, :] = swiglu(h_scr[r * rc:(r + 1) * rc, :])

    @pl.when((j > 0) & (j < nj - 1))
    def _():
        o_ref[...] += swiglu(h_scr[...])

    @pl.when(j == nj - 1)
    def _():
        gate = mod_ref[3 * sl + 2:3 * sl + 3, :]
        o_ref[...] = x_ref[...] + (MACARON_WEIGHT * gate) * (o_ref[...] + swiglu(h_scr[...]))


def _ffn(xv, mod, norm_g, w_in, w_out, next_w, *, layer, sl, B, S, tm, tf):
    D = norm_g.shape[-1]
    F = w_out.shape[0]
    tm = min(tm, S)
    nt, nj = S // tm, F // tf
    assert nj >= 2, "first and last F steps are distinct code paths"
    rows = lambda b, ti, j: (b * nt + ti, 0)
    in_specs = [
        pl.BlockSpec((tm, D), rows),
        pl.BlockSpec((None, None, N_SUBLAYERS * 3, D), lambda b, ti, j: (layer, b, 0, 0)),
        pl.BlockSpec((None, None, 1, D), lambda b, ti, j: (layer, sl, 0, 0)),
        pl.BlockSpec((D, tf), lambda b, ti, j: (0, j)),
        pl.BlockSpec((D, tf), lambda b, ti, j: (0, nj + j)),
        pl.BlockSpec((tf, D), lambda b, ti, j: (j, 0)),
    ]
    operands = [xv, mod, norm_g.reshape(norm_g.shape[0], N_SUBLAYERS, 1, D), w_in, w_in, w_out]
    out_specs = [pl.BlockSpec((tm, D), rows)]
    out_shape = [jax.ShapeDtypeStruct((B * S, D), F32)]
    if next_w is not None:
        nwi, nwo, l2, w2 = next_w
        steps = B * nt * nj
        assert D % (B * nt) == 0 and (2 * F) % nj == 0 and F % steps == 0
        rin, cin, rout = D // (B * nt), 2 * F // nj, F // steps
        in_specs += [
            pl.BlockSpec((None, None, rin, cin), lambda b, ti, j: (l2, w2, b * nt + ti, j)),
            pl.BlockSpec((None, None, rout, D), lambda b, ti, j: (l2, w2, (b * nt + ti) * nj + j, 0)),
        ]
        operands += [nwi, nwo]
        out_specs += [
            pl.BlockSpec((rin, cin), lambda b, ti, j: (b * nt + ti, j)),
            pl.BlockSpec((rout, D), lambda b, ti, j: ((b * nt + ti) * nj + j, 0)),
        ]
        out_shape += [jax.ShapeDtypeStruct((D, 2 * F), BF16), jax.ShapeDtypeStruct((F, D), BF16)]
    vmem = 4 * tm * D * 4 + tm * D * 2 + 2 * 3 * D * tf * 2 + 4 * tm * tf * 4 + (6 << 20)
    out = pl.pallas_call(
        functools.partial(_ffn_body, sl=sl, cast_next=next_w is not None),
        out_shape=out_shape,
        grid=(B, nt, nj),
        in_specs=in_specs,
        out_specs=out_specs,
        scratch_shapes=[pltpu.VMEM((tm, D), BF16)],
        compiler_params=_compiler_params(3, vmem),
        name=f"ffn_l{layer}_s{sl}",
    )(*operands)
    return out[0], tuple(out[1:])


def _qkv_body(x_ref, pos_ref, invf_ref, mod_ref, g_ref, w_ref, gain_ref, o_ref, *, tn, q_mult):
    tm, D = x_ref.shape
    half = ROT_DIM // 2
    gains = (gain_ref[0] * q_mult, gain_ref[1])
    rc = tm // QKV_ROW_CHUNKS

    for rows in (slice(r * rc, (r + 1) * rc) for r in range(QKV_ROW_CHUNKS)):
        h = _norm_mod(x_ref[rows, :], g_ref[...], mod_ref[3:4, :], mod_ref[4:5, :]).astype(BF16)
        ang = pos_ref[rows, :].astype(F32) * invf_ref[...]
        c, s = jnp.cos(ang), jnp.sin(ang)
        lane = lax.broadcasted_iota(jnp.int32, ang.shape, 1)
        cos = jnp.where(lane < ROT_DIM, c, 1.0)
        sin_a = jnp.where(lane < half, -s, 0.0)
        sin_b = jnp.where((lane >= half) & (lane < ROT_DIM), s, 0.0)

        for jn in range(w_ref.shape[-1] // tn):
            cols = slice(jn * tn, (jn + 1) * tn)
            acc = jnp.dot(h, w_ref[:, cols], preferred_element_type=F32)
            which = (jn * tn) // D
            if which == 2:
                o_ref[rows, cols] = acc.astype(BF16)
                continue
            for ch in range(tn // HEAD_DIM):
                t = acc[:, ch * HEAD_DIM:(ch + 1) * HEAD_DIM]
                ms = jnp.mean(t * t, axis=-1, keepdims=True)
                y = t * lax.rsqrt(ms + NORM_EPS) * gains[which]
                r = (y * cos + pltpu.roll(y, HEAD_DIM - half, 1) * sin_a
                     + pltpu.roll(y, half, 1) * sin_b)
                lo = jn * tn + ch * HEAD_DIM
                o_ref[rows, lo:lo + HEAD_DIM] = r.astype(BF16)


def _qkv(xv, pos, mod, norm_g, w, q_gain, k_gain, *, layer, j, B, S, tm, tn):
    D = norm_g.shape[-1]
    N = w.shape[-1]
    tm = min(tm, S)
    nt = S // tm
    half = ROT_DIM // 2
    inv_freq = ROPE_THETA ** (-np.arange(0, ROT_DIM, 2, dtype=np.float64) / ROT_DIM)
    invf = jnp.asarray(np.tile(inv_freq, HEAD_DIM // half)[None, :], dtype=F32)
    gains = jnp.stack([q_gain, k_gain], axis=1).reshape(-1, 2, 1, HEAD_DIM)
    q_mult = HEAD_DIM ** -0.5 * math.log2(math.e)
    rows = lambda b, ti: (b * nt + ti, 0)
    vmem = 2 * tm * D * 4 + tm * D * 2 + D * N * 2 + 2 * tm * N * 2 + 4 * tm * tn * 4 \
        + 8 * tm * V7X_LANES * 4 + (6 << 20)
    return pl.pallas_call(
        functools.partial(_qkv_body, tn=tn, q_mult=q_mult),
        out_shape=jax.ShapeDtypeStruct((B * S, N), BF16),
        grid=(B, nt),
        in_specs=[
            pl.BlockSpec((tm, D), rows),
            pl.BlockSpec((tm, 1), rows),
            pl.BlockSpec((1, HEAD_DIM), lambda b, ti: (0, 0)),
            pl.BlockSpec((None, None, N_SUBLAYERS * 3, D), lambda b, ti: (layer, b, 0, 0)),
            pl.BlockSpec((None, None, 1, D), lambda b, ti: (layer, 1, 0, 0)),
            _resident((None, D, N), lambda b, ti: (j, 0, 0)),
            pl.BlockSpec((None, 2, 1, HEAD_DIM), lambda b, ti: (j, 0, 0, 0)),
        ],
        out_specs=pl.BlockSpec((tm, N), rows),
        compiler_params=_compiler_params(2, vmem),
        name=f"qkv_l{layer}",
    )(xv, pos, invf, mod, norm_g.reshape(norm_g.shape[0], N_SUBLAYERS, 1, D), w, gains)


def _attn_body(q_ref, k_ref, v_ref, lam_ref, sub_ref, o_ref, m_scr, l_scr, acc_scr,
               sa_scr, sb_scr, *, tq, lambda_init):
    qi = pl.program_id(2)
    tk = tq // 2
    m_scr[...] = jnp.full(m_scr.shape, NEG_BIG, F32)
    l_scr[...] = jnp.zeros(l_scr.shape, F32)
    acc_scr[...] = jnp.zeros(acc_scr.shape, F32)
    q = q_ref[...]
    key = lax.broadcasted_iota(jnp.int32, (tk, tq), 0)
    qry = lax.broadcasted_iota(jnp.int32, (tk, tq), 1)
    causal = key <= qry

    def scores(kb, s_scr, lo=0):
        kblk = k_ref[pl.ds(pl.multiple_of(kb * tk, tk), tk), :]
        for c in range(2):
            qc = q[lo:, c * HEAD_DIM:(c + 1) * HEAD_DIM]
            kc = kblk[:, c * HEAD_DIM:(c + 1) * HEAD_DIM]
            s_scr[c, :, lo:] = lax.dot_general(kc, qc, (((1,), (1,)), ((), ())),
                                               preferred_element_type=F32)

    def consume(kb, s_scr, lo=0, masked=False):
        vblk = v_ref[pl.ds(pl.multiple_of(kb * tk, tk), tk), :]
        for c in range(2):
            s = s_scr[c, :, lo:]
            if masked:
                s = jnp.where(causal[:, :tq - lo], s, NEG_BIG)
            m_prev = m_scr[c, :, lo:]
            m_new = jnp.maximum(m_prev, jnp.max(s, axis=0, keepdims=True))
            p = jnp.exp2(s - m_new)
            alpha = jnp.exp2(m_prev - m_new)
            l_scr[c, :, lo:] = alpha * l_scr[c, :, lo:] + jnp.sum(p, axis=0, keepdims=True)
            pv = lax.dot_general(vblk, p.astype(BF16), (((0,), (0,)), ((), ())),
                                 preferred_element_type=F32)
            acc_scr[c, :, lo:] = alpha * acc_scr[c, :, lo:] + pv
            m_scr[c, :, lo:] = m_new

    scores(0, sa_scr)

    def pair(p, carry):
        scores(2 * p + 1, sb_scr)
        consume(2 * p, sa_scr)
        scores(2 * p + 2, sa_scr)
        consume(2 * p + 1, sb_scr)
        return carry

    lax.fori_loop(0, qi, pair, 0)
    scores(2 * qi + 1, sb_scr, lo=tk)
    consume(2 * qi, sa_scr, masked=True)
    consume(2 * qi + 1, sb_scr, lo=tk, masked=True)

    lv = lam_ref[...]
    lam = (jnp.exp(jnp.sum(lv[0:1] * lv[1:2], axis=-1, keepdims=True))
           - jnp.exp(jnp.sum(lv[2:3] * lv[3:4], axis=-1, keepdims=True)) + lambda_init)
    o = acc_scr[0] / l_scr[0] - lam * (acc_scr[1] / l_scr[1])
    ms = jnp.mean(o * o, axis=0, keepdims=True)
    o = (o * lax.rsqrt(ms + NORM_EPS)).T
    o_ref[...] = ((o * sub_ref[...]) * (1.0 - lambda_init)).astype(BF16)


def _attn(qkv, lam, subln, *, j, B, S, D, tq, lambda_init):
    H = D // V_DIM
    tq = min(tq, S)
    nq = S // tq
    vmem = 2 * 2 * S * V_DIM * 2 + 4 * tq * V_DIM * 2 + 2 * tq * V_DIM * 4 \
        + 4 * tq * V7X_SUBLANES * 4 + 6 * tq * tq * 4 + (6 << 20)
    return pl.pallas_call(
        functools.partial(_attn_body, tq=tq, lambda_init=lambda_init),
        out_shape=jax.ShapeDtypeStruct((B * S, D), BF16),
        grid=(B, H, nq),
        in_specs=[
            pl.BlockSpec((tq, V_DIM), lambda b, h, qi: (b * nq + qi, h)),
            pl.BlockSpec((S, V_DIM), lambda b, h, qi: (b, H + h)),
            pl.BlockSpec((S, V_DIM), lambda b, h, qi: (b, 2 * H + h)),
            pl.BlockSpec((None, 4, HEAD_DIM), lambda b, h, qi: (j, 0, 0)),
            pl.BlockSpec((None, 1, V_DIM), lambda b, h, qi: (j, 0, 0)),
        ],
        out_specs=pl.BlockSpec((tq, V_DIM), lambda b, h, qi: (b * nq + qi, h)),
        scratch_shapes=[pltpu.VMEM((2, 1, tq), F32), pltpu.VMEM((2, 1, tq), F32),
                        pltpu.VMEM((2, V_DIM, tq), F32),
                        pltpu.VMEM((2, tq // 2, tq), F32), pltpu.VMEM((2, tq // 2, tq), F32)],
        compiler_params=_compiler_params(3, vmem),
        name="diff_attn",
    )(qkv, qkv, qkv, lam, subln.reshape(-1, 1, V_DIM))


def _oproj_body(x_ref, a_ref, w_ref, mod_ref, o_ref):
    m = jnp.dot(a_ref[...], w_ref[...], preferred_element_type=F32)
    o_ref[...] = x_ref[...] + mod_ref[5:6, :] * m


def _oproj(xv, a, w, mod, *, layer, j, B, S, tm):
    D = xv.shape[-1]
    tm = min(tm, S)
    nt = S // tm
    rows = lambda b, ti: (b * nt + ti, 0)
    vmem = 4 * tm * D * 4 + 2 * tm * D * 2 + D * D * 2 + tm * D * 4 + (6 << 20)
    return pl.pallas_call(
        _oproj_body,
        out_shape=jax.ShapeDtypeStruct((B * S, D), F32),
        grid=(B, nt),
        in_specs=[
            pl.BlockSpec((tm, D), rows),
            pl.BlockSpec((tm, D), rows),
            _resident((None, D, D), lambda b, ti: (j, 0, 0)),
            pl.BlockSpec((None, None, N_SUBLAYERS * 3, D), lambda b, ti: (layer, b, 0, 0)),
        ],
        out_specs=pl.BlockSpec((tm, D), rows),
        compiler_params=_compiler_params(2, vmem),
        name=f"attn_out_l{layer}",
    )(xv, a, w, mod)


def _ssm_prep_body(ar_ref, ai_ref, ls_ref, arr_ref, air_ref, lsr_ref, br_ref, bi_ref,
                   abr_ref, abi_ref, a2r_ref, a2i_ref, bbr_ref, bbi_ref, abbr_ref, abbi_ref):
    def abar(ar, ai, ls):
        dt = jnp.exp(ls)
        mag = jnp.exp(dt * ar)
        return mag * jnp.cos(dt * ai), mag * jnp.sin(dt * ai)

    re, im = abar(ar_ref[...], ai_ref[...], ls_ref[...])
    abr_ref[...] = re
    abi_ref[...] = im
    a2r_ref[...] = re * re - im * im
    a2i_ref[...] = 2.0 * (re * im)

    ar, ai = arr_ref[...], air_ref[...]
    re, im = abar(ar, ai, lsr_ref[...])
    den = ar * ar + ai * ai
    num_re = re - 1.0
    coef_re = (num_re * ar + im * ai) / den
    coef_im = (im * ar - num_re * ai) / den
    br, bi = br_ref[...], bi_ref[...]
    bbr = coef_re * br - coef_im * bi
    bbi = coef_re * bi + coef_im * br
    bbr_ref[...] = bbr
    bbi_ref[...] = bbi
    abbr_ref[...] = re * bbr - im * bbi
    abbi_ref[...] = re * bbi + im * bbr


def _ssm_prep(a_re, a_im, log_step, b_re, b_im):
    G, N = a_re.shape
    P = b_re.shape[-1]
    rep = lambda v: jnp.repeat(v, P, axis=0)
    flat = lambda v: jnp.swapaxes(v, 1, 2).reshape(G * P, N)
    ls = log_step.reshape(G, 1)
    small = jax.ShapeDtypeStruct((G, N), F32)
    big = jax.ShapeDtypeStruct((G * P, N), F32)
    return pl.pallas_call(
        _ssm_prep_body,
        out_shape=(small, small, small, small, big, big, big, big),
        name="ssm_discretise",
    )(a_re, a_im, ls, rep(a_re), rep(a_im), rep(ls), flat(b_re), flat(b_im))


def _ssm_block_operands(abr, abi, a2r, a2i, bbr, bbi, abbr, abbi, c_re, c_im):
    G, N = abr.shape
    P = SSM_GROUP
    gb = SSM_GROUPS_PER_BLOCK
    nb = G // gb
    eye = jnp.eye(gb, dtype=F32)

    def b_block(v):
        v = v.reshape(nb, gb, P, 1, N) * eye[None, :, None, :, None]
        return v.reshape(nb, gb * P, gb * N)

    def c_block(v):
        v = jnp.swapaxes(v.reshape(nb, gb, P, N), 2, 3)
        v = v.reshape(nb, gb, N, 1, P) * eye[None, :, None, :, None]
        return v.reshape(nb, gb * N, gb * P)

    bcur = jnp.concatenate([b_block(bbr), b_block(bbi)], axis=-1)
    bprev = jnp.concatenate([b_block(abbr), b_block(abbi)], axis=-1)
    bcat = jnp.concatenate([bcur, bprev], axis=1).astype(BF16)

    c8 = jnp.concatenate([c_block(c_re), -c_block(c_im)], axis=1)
    npair = SSM_BLOCKS_PER_READOUT
    eye2 = jnp.eye(npair, dtype=F32)
    cpair = c8.reshape(nb // npair, npair, 2 * gb * N, 1, gb * P) * eye2[None, :, None, :, None]
    cpair = cpair.reshape(nb // npair, npair * 2 * gb * N, npair * gb * P).astype(BF16)

    first = (lax.broadcasted_iota(jnp.int32, (1, 1, V7X_SUBLANES, 1), 2) < V7X_SUBLANES // 2)
    a1 = jnp.stack([abr, abi], axis=0).reshape(2, nb, 1, gb * N).swapaxes(0, 1)
    a2 = jnp.stack([a2r, a2i], axis=0).reshape(2, nb, 1, gb * N).swapaxes(0, 1)
    a_carry = jnp.where(first, a1, a2)
    return bcat, cpair, a_carry


def _gelu_tanh(x):
    c = math.sqrt(2.0 / math.pi)
    return x * (0.5 * (1.0 + jnp.tanh(c * (x + 0.044715 * (x * x * x)))))


def _ssm_body(x_ref, mod_ref, g_ref, d_ref, bcat_ref, cpair_ref, acar_ref, z_ref,
              hs_scr, hcat_scr, st_scr, y_scr, carry_scr, *, tt, nb):
    i = pl.program_id(0)
    B, _, D = x_ref.shape
    tm = tt * B
    sub = V7X_SUBLANES
    ns = SSM_BLOCK_STATES
    kin = 2 * V7X_LANES
    npair = SSM_BLOCKS_PER_READOUT
    cout = npair * V7X_LANES
    slab = lambda k: slice(k * V7X_LANES, (k + 1) * V7X_LANES)

    @pl.when(i == 0)
    def _():
        carry_scr[...] = jnp.zeros(carry_scr.shape, F32)

    for b in range(B):
        h = _norm_mod(x_ref[b], g_ref[...], mod_ref[b, 3:4, :], mod_ref[b, 4:5, :])
        for k in range(nb):
            hs_scr[k, pl.ds(b, tt, stride=B), :] = h[:, slab(k)]
    second = (lax.broadcasted_iota(jnp.int32, (tm, V7X_LANES), 0) & (sub // 2)) != 0
    for k in range(nb):
        hk = hs_scr[k]
        hcat_scr[:, k * kin:k * kin + V7X_LANES] = hk.astype(BF16)
        hcat_scr[:, k * kin + V7X_LANES:(k + 1) * kin] = jnp.where(
            second, pltpu.roll(hk, sub // 2, 0), 0.0).astype(BF16)

    first = lax.broadcasted_iota(jnp.int32, (sub, ns), 0) < sub // 2
    re_sl, im_sl = slice(0, ns), slice(ns, 2 * ns)

    for sb in range(nb):
        buf = sb % SSM_STATE_BUFFERS
        st_scr[buf] = jnp.dot(hcat_scr[:, sb * kin:(sb + 1) * kin], bcat_ref[sb],
                              preferred_element_type=F32)
        ar, ai = acar_ref[sb, 0], acar_ref[sb, 1]

        def step(r, carry, buf=buf, ar=ar, ai=ai):
            hr, hi = carry
            rows = pl.ds(pl.multiple_of(r * sub, sub), sub)
            sr = st_scr[buf, rows, re_sl] + (hr * ar - hi * ai)
            si = st_scr[buf, rows, im_sl] + (hr * ai + hi * ar)
            st_scr[buf, rows, re_sl] = sr
            st_scr[buf, rows, im_sl] = si
            return (jnp.where(first, pltpu.roll(sr, sub // 2, 0), sr),
                    jnp.where(first, pltpu.roll(si, sub // 2, 0), si))

        hr, hi = lax.fori_loop(0, tm // sub, step, (carry_scr[sb, 0], carry_scr[sb, 1]),
                               unroll=True)
        carry_scr[sb, 0] = hr
        carry_scr[sb, 1] = hi

        if sb % npair == npair - 1:
            pair = sb // npair
            acc = None
            for e in range(npair):
                states = st_scr[(sb - npair + 1 + e) % SSM_STATE_BUFFERS].astype(BF16)
                part = jnp.dot(states, cpair_ref[pair, e * 2 * ns:(e + 1) * 2 * ns, :],
                               preferred_element_type=F32)
                acc = part if acc is None else acc + part
            y_scr[:, pair * cout:(pair + 1) * cout] = acc

    for k in range(nb):
        hs_scr[k] = _gelu_tanh(y_scr[:, slab(k)] + d_ref[:, slab(k)] * hs_scr[k])
    for b in range(B):
        for k in range(nb):
            z_ref[b, :, slab(k)] = hs_scr[k, pl.ds(b, tt, stride=B), :].astype(BF16)


def _ssm(xv, mod, norm_g, d_skip, bcat, cpair, a_carry, *, layer, j, B, S, tt):
    D = norm_g.shape[-1]
    assert V7X_SUBLANES == 2 * B, "the scan packs 2 timesteps x B batches per 8-sublane row-group"
    nb = bcat.shape[0]
    assert nb == D // V7X_LANES
    tt = min(tt, S)
    tm = tt * B
    ns2 = 2 * SSM_BLOCK_STATES
    const = lambda *shape: _resident(shape, lambda i: (0,) * len(shape))
    vmem = 2 * tm * D * 4 + 2 * tm * D * 2 + 2 * tm * D * 4 + 2 * tm * D * 2 \
        + (SSM_STATE_BUFFERS + 2) * tm * ns2 * 4 + (bcat.size + cpair.size) * 2 + 2 * tm * D * 4 \
        + (6 << 20)
    return pl.pallas_call(
        functools.partial(_ssm_body, tt=tt, nb=nb),
        out_shape=jax.ShapeDtypeStruct((B, S, D), BF16),
        grid=(S // tt,),
        in_specs=[
            pl.BlockSpec((B, tt, D), lambda i: (0, i, 0)),
            pl.BlockSpec((None, B, N_SUBLAYERS * 3, D), lambda i: (layer, 0, 0, 0)),
            pl.BlockSpec((None, None, 1, D), lambda i: (layer, 1, 0, 0)),
            pl.BlockSpec((None, 1, D), lambda i: (j, 0, 0)),
            const(*bcat.shape),
            const(*cpair.shape),
            const(*a_carry.shape),
        ],
        out_specs=pl.BlockSpec((B, tt, D), lambda i: (0, i, 0)),
        scratch_shapes=[
            pltpu.VMEM((nb, tm, V7X_LANES), F32),
            pltpu.VMEM((tm, 2 * D), BF16),
            pltpu.VMEM((SSM_STATE_BUFFERS, tm, ns2), F32),
            pltpu.VMEM((tm, D), F32),
            pltpu.VMEM((nb, 2, V7X_SUBLANES, SSM_BLOCK_STATES), F32),
        ],
        compiler_params=_compiler_params(1, vmem),
        name=f"ssm_l{layer}",
    )(xv.reshape(B, S, D), mod, norm_g.reshape(norm_g.shape[0], N_SUBLAYERS, 1, D),
      d_skip.reshape(-1, 1, D), bcat, cpair, a_carry)


def _glu_body(x_ref, z_ref, wv_ref, wg_ref, mod_ref, o_ref):
    z = z_ref[...]
    val = jnp.dot(z, wv_ref[...], preferred_element_type=F32)
    gt = jnp.dot(z, wg_ref[...], preferred_element_type=F32)
    o_ref[...] = x_ref[...] + mod_ref[5:6, :] * (val * jax.nn.sigmoid(gt))


def _glu(xv, zv, w, mod, *, layer, j, B, S, tm, tn):
    D = w.shape[1]
    tm = min(tm, S)
    nt, nn = S // tm, D // tn
    vmem = 4 * tm * tn * 4 + 2 * tm * D * 2 + 4 * D * tn * 2 + 3 * tm * tn * 4 + (6 << 20)
    return pl.pallas_call(
        _glu_body,
        out_shape=jax.ShapeDtypeStruct((B * S, D), F32),
        grid=(nn, B, nt),
        in_specs=[
            pl.BlockSpec((tm, tn), lambda jn, b, ti: (b * nt + ti, jn)),
            pl.BlockSpec((tm, D), lambda jn, b, ti: (b * nt + ti, 0)),
            pl.BlockSpec((None, D, tn), lambda jn, b, ti: (j, 0, jn)),
            pl.BlockSpec((None, D, tn), lambda jn, b, ti: (j, 0, nn + jn)),
            pl.BlockSpec((None, None, N_SUBLAYERS * 3, tn), lambda jn, b, ti: (layer, b, 0, jn)),
        ],
        out_specs=pl.BlockSpec((tm, tn), lambda jn, b, ti: (b * nt + ti, jn)),
        compiler_params=_compiler_params(3, vmem),
        name=f"glu_l{layer}",
    )(xv, zv, w, w, mod)


def kernel(x, c, positions, norm_g, ada_w, ada_b, ffn_w_in, ffn_w_out, attn_w_in, attn_w_out,
           attn_q_norm, attn_k_norm, attn_lambda, attn_subln, ssm_a_re, ssm_a_im, ssm_log_step,
           ssm_b_re, ssm_b_im, ssm_c_re, ssm_c_im, ssm_d, ssm_w_glu):
    B, S, D = x.shape
    depth = norm_g.shape[0]

    mod = _adaln(c, ada_w, ada_b)
    attn_w_in = attn_w_in.astype(BF16)
    attn_w_out = attn_w_out.astype(BF16)
    ssm_w_glu = ssm_w_glu.astype(BF16)
    pos = positions.reshape(B * S, 1)

    ffn_order = [(i, which) for i in range(depth) for which in range(2)]
    ffn_w = (ffn_w_in[0, 0].astype(BF16), ffn_w_out[0, 0].astype(BF16))

    def ffn(xv, ffn_w, n):
        layer, which = ffn_order[n]
        nxt = (ffn_w_in, ffn_w_out) + ffn_order[n + 1] if n + 1 < len(ffn_order) else None
        return _ffn(xv, mod, norm_g, *ffn_w, nxt, layer=layer, sl=2 * which, B=B, S=S,
                    tm=FFN_ROWS, tf=FFN_F_TILE)

    xv = x.reshape(B * S, D)
    for i in range(depth):
        j = i // N_MIXERS
        xv, ffn_w = ffn(xv, ffn_w, 2 * i)
        if i % N_MIXERS == 0:
            lambda_init = 0.8 - 0.6 * math.exp(-0.3 * i)
            qkv = _qkv(xv, pos, mod, norm_g, attn_w_in, attn_q_norm, attn_k_norm,
                       layer=i, j=j, B=B, S=S, tm=QKV_ROWS, tn=QKV_COL_CHUNK)
            att = _attn(qkv, attn_lambda, attn_subln, j=j, B=B, S=S, D=D, tq=ATTN_QUERY_ROWS,
                        lambda_init=lambda_init)
            xv = _oproj(xv, att, attn_w_out, mod, layer=i, j=j, B=B, S=S, tm=OPROJ_ROWS)
        else:
            disc = _ssm_prep(ssm_a_re[j], ssm_a_im[j], ssm_log_step[j], ssm_b_re[j], ssm_b_im[j])
            operands = _ssm_block_operands(*disc, ssm_c_re[j], ssm_c_im[j])
            z = _ssm(xv, mod, norm_g, ssm_d, *operands, layer=i, j=j, B=B, S=S,
                     tt=SSM_TIMESTEPS)
            xv = _glu(xv, z.reshape(B * S, D), ssm_w_glu, mod, layer=i, j=j, B=B, S=S,
                      tm=GLU_ROWS, tn=GLU_COL_TILE)
        xv, ffn_w = ffn(xv, ffn_w, 2 * i + 1)
    return xv.reshape(B, S, D)
```

```python
import functools
import math

import numpy as np
import jax
import jax.numpy as jnp
from jax import lax
from jax.experimental import pallas as pl
from jax.experimental.pallas import tpu as pltpu

HEAD_DIM = 128
V_DIM = 2 * HEAD_DIM
ROT_DIM = HEAD_DIM // 4
ROPE_THETA = 500000.0
SSM_GROUP = 16
SSM_STATE = 64
N_SUBLAYERS = 3
N_MIXERS = 2
NORM_EPS = 1e-6
MACARON_WEIGHT = 0.5

V7X_LANES = 128
V7X_SUBLANES = 8
V7X_MXU_DIM = 256
V7X_SCOPED_VMEM_CAP = 58 * 1024 * 1024

F32 = jnp.float32
BF16 = jnp.bfloat16
NEG_BIG = -1e30

SSM_GROUPS_PER_BLOCK = V7X_LANES // SSM_GROUP
SSM_BLOCK_STATES = SSM_GROUPS_PER_BLOCK * SSM_STATE
SSM_BLOCKS_PER_READOUT = V7X_MXU_DIM // V7X_LANES
SSM_STATE_BUFFERS = 2 * SSM_BLOCKS_PER_READOUT

FFN_ROWS, FFN_F_TILE = 1024, 512
QKV_ROWS, QKV_COL_CHUNK = 512, 512
ATTN_QUERY_ROWS = 1024
OPROJ_ROWS = 512
SSM_TIMESTEPS = 64
GLU_ROWS, GLU_COL_TILE = 1024, 1024

QKV_ROW_CHUNKS = 2
FFN_FIRST_STEP_CHUNKS = 4
FFN_NORM_ROWS = 64


def _compiler_params(n_axes, vmem_bytes):
    limit = min(int(vmem_bytes), V7X_SCOPED_VMEM_CAP)
    return pltpu.CompilerParams(dimension_semantics=("arbitrary",) * n_axes,
                                vmem_limit_bytes=limit)


def _resident(block_shape, index_map):
    return pl.BlockSpec(block_shape, index_map, pipeline_mode=pl.Buffered(1))


def _norm_mod(x, g, shift, scale):
    ms = jnp.mean(x * x, axis=-1, keepdims=True)
    y = x * lax.rsqrt(ms + NORM_EPS) * g
    return y * (1.0 + scale) + shift


def _adaln_body(c_ref, w_ref, b_ref, o_ref):
    c = c_ref[...]
    cond = (c * jax.nn.sigmoid(c)).astype(BF16)
    o_ref[...] = jnp.dot(cond, w_ref[...].astype(BF16), preferred_element_type=F32) + b_ref[...]


def _adaln(c, ada_w, ada_b):
    L, D, N = ada_w.shape
    B = c.shape[0]
    rows = V7X_SUBLANES * ((B + V7X_SUBLANES - 1) // V7X_SUBLANES)
    c_pad = jnp.pad(c, ((0, rows - B), (0, 0)))
    tn = D
    out = pl.pallas_call(
        _adaln_body,
        out_shape=jax.ShapeDtypeStruct((L, rows, N), F32),
        grid=(L, N // tn),
        in_specs=[
            pl.BlockSpec((rows, D), lambda l, j: (0, 0)),
            pl.BlockSpec((None, D, tn), lambda l, j: (l, 0, j)),
            pl.BlockSpec((None, 1, tn), lambda l, j: (l, 0, j)),
        ],
        out_specs=pl.BlockSpec((None, rows, tn), lambda l, j: (l, 0, j)),
        compiler_params=_compiler_params(2, 2 * D * tn * 4 + D * tn * 2 + (8 << 20)),
        name="adaln",
    )(c_pad, ada_w, ada_b.reshape(L, 1, N))
    return out[:, :B].reshape(L, B, N_SUBLAYERS * 3, D)


def _ffn_body(*refs, sl, cast_next):
    if cast_next:
        (x_ref, mod_ref, g_ref, wa_ref, wb_ref, wo_ref, nwi_ref, nwo_ref,
         o_ref, nwi_out_ref, nwo_out_ref, h_scr) = refs
        nwi_out_ref[...] = nwi_ref[...].astype(BF16)
        nwo_out_ref[...] = nwo_ref[...].astype(BF16)
    else:
        x_ref, mod_ref, g_ref, wa_ref, wb_ref, wo_ref, o_ref, h_scr = refs
    j = pl.program_id(2)
    nj = pl.num_programs(2)
    tm = x_ref.shape[0]

    def swiglu(h):
        a = jnp.dot(h, wa_ref[...], preferred_element_type=F32)
        b = jnp.dot(h, wb_ref[...], preferred_element_type=F32)
        act = (a * jax.nn.sigmoid(a) * b).astype(BF16)
        return jnp.dot(act, wo_ref[...], preferred_element_type=F32)

    @pl.when(j == 0)
    def _():
        shift, scale = mod_ref[3 * sl:3 * sl + 1, :], mod_ref[3 * sl + 1:3 * sl + 2, :]
        rc = tm // FFN_FIRST_STEP_CHUNKS
        nr = min(FFN_NORM_ROWS, rc)
        for r in range(FFN_FIRST_STEP_CHUNKS):
            for lo in range(r * rc, (r + 1) * rc, nr):
                h_scr[lo:lo + nr, :] = _norm_mod(x_ref[lo:lo + nr, :], g_ref[...], shift,
                                                 scale).astype(BF16)
            o_ref[r * rc:(r + 1) * rc, :] = swiglu(h_scr[r * rc:(r + 1) * rc, :])

    @pl.when((j > 0) & (j < nj - 1))
    def _():
        o_ref[...] += swiglu(h_scr[...])

    @pl.when(j == nj - 1)
    def _():
        gate = mod_ref[3 * sl + 2:3 * sl + 3, :]
        o_ref[...] = x_ref[...] + (MACARON_WEIGHT * gate) * (o_ref[...] + swiglu(h_scr[...]))


def _ffn(xv, mod, norm_g, w_in, w_out, next_w, *, layer, sl, B, S, tm, tf):
    D = norm_g.shape[-1]
    F = w_out.shape[0]
    tm = min(tm, S)
    nt, nj = S // tm, F // tf
    assert nj >= 2, "first and last F steps are distinct code paths"
    rows = lambda b, ti, j: (b * nt + ti, 0)
    in_specs = [
        pl.BlockSpec((tm, D), rows),
        pl.BlockSpec((None, None, N_SUBLAYERS * 3, D), lambda b, ti, j: (layer, b, 0, 0)),
        pl.BlockSpec((None, None, 1, D), lambda b, ti, j: (layer, sl, 0, 0)),
        pl.BlockSpec((D, tf), lambda b, ti, j: (0, j)),
        pl.BlockSpec((D, tf), lambda b, ti, j: (0, nj + j)),
        pl.BlockSpec((tf, D), lambda b, ti, j: (j, 0)),
    ]
    operands = [xv, mod, norm_g.reshape(norm_g.shape[0], N_SUBLAYERS, 1, D), w_in, w_in, w_out]
    out_specs = [pl.BlockSpec((tm, D), rows)]
    out_shape = [jax.ShapeDtypeStruct((B * S, D), F32)]
    if next_w is not None:
        nwi, nwo, l2, w2 = next_w
        steps = B * nt * nj
        assert D % (B * nt) == 0 and (2 * F) % nj == 0 and F % steps == 0
        rin, cin, rout = D // (B * nt), 2 * F // nj, F // steps
        in_specs += [
            pl.BlockSpec((None, None, rin, cin), lambda b, ti, j: (l2, w2, b * nt + ti, j)),
            pl.BlockSpec((None, None, rout, D), lambda b, ti, j: (l2, w2, (b * nt + ti) * nj + j, 0)),
        ]
        operands += [nwi, nwo]
        out_specs += [
            pl.BlockSpec((rin, cin), lambda b, ti, j: (b * nt + ti, j)),
            pl.BlockSpec((rout, D), lambda b, ti, j: ((b * nt + ti) * nj + j, 0)),
        ]
        out_shape += [jax.ShapeDtypeStruct((D, 2 * F), BF16), jax.ShapeDtypeStruct((F, D), BF16)]
    vmem = 4 * tm * D * 4 + tm * D * 2 + 2 * 3 * D * tf * 2 + 4 * tm * tf * 4 + (6 << 20)
    out = pl.pallas_call(
        functools.partial(_ffn_body, sl=sl, cast_next=next_w is not None),
        out_shape=out_shape,
        grid=(B, nt, nj),
        in_specs=in_specs,
        out_specs=out_specs,
        scratch_shapes=[pltpu.VMEM((tm, D), BF16)],
        compiler_params=_compiler_params(3, vmem),
        name=f"ffn_l{layer}_s{sl}",
    )(*operands)
    return out[0], tuple(out[1:])


def _qkv_body(x_ref, pos_ref, invf_ref, mod_ref, g_ref, w_ref, gain_ref, o_ref, *, tn, q_mult):
    tm, D = x_ref.shape
    half = ROT_DIM // 2
    gains = (gain_ref[0] * q_mult, gain_ref[1])
    rc = tm // QKV_ROW_CHUNKS

    for rows in (slice(r * rc, (r + 1) * rc) for r in range(QKV_ROW_CHUNKS)):
        h = _norm_mod(x_ref[rows, :], g_ref[...], mod_ref[3:4, :], mod_ref[4:5, :]).astype(BF16)
        ang = pos_ref[rows, :].astype(F32) * invf_ref[...]
        c, s = jnp.cos(ang), jnp.sin(ang)
        lane = lax.broadcasted_iota(jnp.int32, ang.shape, 1)
        cos = jnp.where(lane < ROT_DIM, c, 1.0)
        sin_a = jnp.where(lane < half, -s, 0.0)
        sin_b = jnp.where((lane >= half) & (lane < ROT_DIM), s, 0.0)

        for jn in range(w_ref.shape[-1] // tn):
            cols = slice(jn * tn, (jn + 1) * tn)
            acc = jnp.dot(h, w_ref[:, cols], preferred_element_type=F32)
            which = (jn * tn) // D
            if which == 2:
                o_ref[rows, cols] = acc.astype(BF16)
                continue
            for ch in range(tn // HEAD_DIM):
                t = acc[:, ch * HEAD_DIM:(ch + 1) * HEAD_DIM]
                ms = jnp.mean(t * t, axis=-1, keepdims=True)
                y = t * lax.rsqrt(ms + NORM_EPS) * gains[which]
                r = (y * cos + pltpu.roll(y, HEAD_DIM - half, 1) * sin_a
                     + pltpu.roll(y, half, 1) * sin_b)
                lo = jn * tn + ch * HEAD_DIM
                o_ref[rows, lo:lo + HEAD_DIM] = r.astype(BF16)


def _qkv(xv, pos, mod, norm_g, w, q_gain, k_gain, *, layer, j, B, S, tm, tn):
    D = norm_g.shape[-1]
    N = w.shape[-1]
    tm = min(tm, S)
    nt = S // tm
    half = ROT_DIM // 2
    inv_freq = ROPE_THETA ** (-np.arange(0, ROT_DIM, 2, dtype=np.float64) / ROT_DIM)
    invf = jnp.asarray(np.tile(inv_freq, HEAD_DIM // half)[None, :], dtype=F32)
    gains = jnp.stack([q_gain, k_gain], axis=1).reshape(-1, 2, 1, HEAD_DIM)
    q_mult = HEAD_DIM ** -0.5 * math.log2(math.e)
    rows = lambda b, ti: (b * nt + ti, 0)
    vmem = 2 * tm * D * 4 + tm * D * 2 + D * N * 2 + 2 * tm * N * 2 + 4 * tm * tn * 4 \
        + 8 * tm * V7X_LANES * 4 + (6 << 20)
    return pl.pallas_call(
        functools.partial(_qkv_body, tn=tn, q_mult=q_mult),
        out_shape=jax.ShapeDtypeStruct((B * S, N), BF16),
        grid=(B, nt),
        in_specs=[
            pl.BlockSpec((tm, D), rows),
            pl.BlockSpec((tm, 1), rows),
            pl.BlockSpec((1, HEAD_DIM), lambda b, ti: (0, 0)),
            pl.BlockSpec((None, None, N_SUBLAYERS * 3, D), lambda b, ti: (layer, b, 0, 0)),
            pl.BlockSpec((None, None, 1, D), lambda b, ti: (layer, 1, 0, 0)),
            _resident((None, D, N), lambda b, ti: (j, 0, 0)),
            pl.BlockSpec((None, 2, 1, HEAD_DIM), lambda b, ti: (j, 0, 0, 0)),
        ],
        out_specs=pl.BlockSpec((tm, N), rows),
        compiler_params=_compiler_params(2, vmem),
        name=f"qkv_l{layer}",
    )(xv, pos, invf, mod, norm_g.reshape(norm_g.shape[0], N_SUBLAYERS, 1, D), w, gains)


def _attn_body(q_ref, k_ref, v_ref, lam_ref, sub_ref, o_ref, m_scr, l_scr, acc_scr,
               sa_scr, sb_scr, *, tq, lambda_init):
    tk = tq // 2
    key = lax.broadcasted_iota(jnp.int32, (tk, tq), 0)
    qry = lax.broadcasted_iota(jnp.int32, (tk, tq), 1)
    causal = key <= qry
    lv = lam_ref[...]
    lam = (jnp.exp(jnp.sum(lv[0:1] * lv[1:2], axis=-1, keepdims=True))
           - jnp.exp(jnp.sum(lv[2:3] * lv[3:4], axis=-1, keepdims=True)) + lambda_init)

    def keys(kb):
        start = kb * tk
        return pl.ds(start if isinstance(start, int) else pl.multiple_of(start, tk), tk)

    for qi in range(q_ref.shape[0] // tq):
        rows = slice(qi * tq, (qi + 1) * tq)
        q = q_ref[rows, :]
        m_scr[...] = jnp.full(m_scr.shape, NEG_BIG, F32)
        l_scr[...] = jnp.zeros(l_scr.shape, F32)
        acc_scr[...] = jnp.zeros(acc_scr.shape, F32)

        def scores(kb, s_scr, lo=0, q=q):
            kblk = k_ref[keys(kb), :]
            for c in range(2):
                qc = q[lo:, c * HEAD_DIM:(c + 1) * HEAD_DIM]
                kc = kblk[:, c * HEAD_DIM:(c + 1) * HEAD_DIM]
                s_scr[c, :, lo:] = lax.dot_general(kc, qc, (((1,), (1,)), ((), ())),
                                                   preferred_element_type=F32)

        def consume(kb, s_scr, lo=0, masked=False):
            vblk = v_ref[keys(kb), :]
            for c in range(2):
                s = s_scr[c, :, lo:]
                if masked:
                    s = jnp.where(causal[:, :tq - lo], s, NEG_BIG)
                m_prev = m_scr[c, :, lo:]
                m_new = jnp.maximum(m_prev, jnp.max(s, axis=0, keepdims=True))
                p = jnp.exp2(s - m_new)
                alpha = jnp.exp2(m_prev - m_new)
                l_scr[c, :, lo:] = alpha * l_scr[c, :, lo:] + jnp.sum(p, axis=0, keepdims=True)
                pv = lax.dot_general(vblk, p.astype(BF16), (((0,), (0,)), ((), ())),
                                     preferred_element_type=F32)
                acc_scr[c, :, lo:] = alpha * acc_scr[c, :, lo:] + pv
                m_scr[c, :, lo:] = m_new

        scores(0, sa_scr)

        def pair(p, carry, scores=scores, consume=consume):
            scores(2 * p + 1, sb_scr)
            consume(2 * p, sa_scr)
            scores(2 * p + 2, sa_scr)
            consume(2 * p + 1, sb_scr)
            return carry

        lax.fori_loop(0, qi, pair, 0)
        scores(2 * qi + 1, sb_scr, lo=tk)
        consume(2 * qi, sa_scr, masked=True)
        consume(2 * qi + 1, sb_scr, lo=tk, masked=True)

        o = acc_scr[0] / l_scr[0] - lam * (acc_scr[1] / l_scr[1])
        ms = jnp.mean(o * o, axis=0, keepdims=True)
        o = (o * lax.rsqrt(ms + NORM_EPS)).T
        o_ref[rows, :] = ((o * sub_ref[...]) * (1.0 - lambda_init)).astype(BF16)


def _attn(qkv, lam, subln, *, j, B, S, D, tq, lambda_init):
    H = D // V_DIM
    tq = min(tq, S)
    vmem = 2 * 4 * S * V_DIM * 2 + 2 * tq * V_DIM * 4 \
        + 4 * tq * V7X_SUBLANES * 4 + 6 * tq * tq * 4 + (6 << 20)
    return pl.pallas_call(
        functools.partial(_attn_body, tq=tq, lambda_init=lambda_init),
        out_shape=jax.ShapeDtypeStruct((B * S, D), BF16),
        grid=(B, H),
        in_specs=[
            pl.BlockSpec((S, V_DIM), lambda b, h: (b, h)),
            pl.BlockSpec((S, V_DIM), lambda b, h: (b, H + h)),
            pl.BlockSpec((S, V_DIM), lambda b, h: (b, 2 * H + h)),
            pl.BlockSpec((None, 4, HEAD_DIM), lambda b, h: (j, 0, 0)),
            pl.BlockSpec((None, 1, V_DIM), lambda b, h: (j, 0, 0)),
        ],
        out_specs=pl.BlockSpec((S, V_DIM), lambda b, h: (b, h)),
        scratch_shapes=[pltpu.VMEM((2, 1, tq), F32), pltpu.VMEM((2, 1, tq), F32),
                        pltpu.VMEM((2, V_DIM, tq), F32),
                        pltpu.VMEM((2, tq // 2, tq), F32), pltpu.VMEM((2, tq // 2, tq), F32)],
        compiler_params=_compiler_params(2, vmem),
        name="diff_attn",
    )(qkv, qkv, qkv, lam, subln.reshape(-1, 1, V_DIM))


def _oproj_body(x_ref, a_ref, w_ref, mod_ref, o_ref):
    m = jnp.dot(a_ref[...], w_ref[...], preferred_element_type=F32)
    o_ref[...] = x_ref[...] + mod_ref[5:6, :] * m


def _oproj(xv, a, w, mod, *, layer, j, B, S, tm):
    D = xv.shape[-1]
    tm = min(tm, S)
    nt = S // tm
    rows = lambda b, ti: (b * nt + ti, 0)
    vmem = 4 * tm * D * 4 + 2 * tm * D * 2 + D * D * 2 + tm * D * 4 + (6 << 20)
    return pl.pallas_call(
        _oproj_body,
        out_shape=jax.ShapeDtypeStruct((B * S, D), F32),
        grid=(B, nt),
        in_specs=[
            pl.BlockSpec((tm, D), rows),
            pl.BlockSpec((tm, D), rows),
            _resident((None, D, D), lambda b, ti: (j, 0, 0)),
            pl.BlockSpec((None, None, N_SUBLAYERS * 3, D), lambda b, ti: (layer, b, 0, 0)),
        ],
        out_specs=pl.BlockSpec((tm, D), rows),
        compiler_params=_compiler_params(2, vmem),
        name=f"attn_out_l{layer}",
    )(xv, a, w, mod)


def _ssm_prep_body(ar_ref, ai_ref, ls_ref, arr_ref, air_ref, lsr_ref, br_ref, bi_ref,
                   abr_ref, abi_ref, a2r_ref, a2i_ref, bbr_ref, bbi_ref, abbr_ref, abbi_ref):
    def abar(ar, ai, ls):
        dt = jnp.exp(ls)
        mag = jnp.exp(dt * ar)
        return mag * jnp.cos(dt * ai), mag * jnp.sin(dt * ai)

    re, im = abar(ar_ref[...], ai_ref[...], ls_ref[...])
    abr_ref[...] = re
    abi_ref[...] = im
    a2r_ref[...] = re * re - im * im
    a2i_ref[...] = 2.0 * (re * im)

    ar, ai = arr_ref[...], air_ref[...]
    re, im = abar(ar, ai, lsr_ref[...])
    den = ar * ar + ai * ai
    num_re = re - 1.0
    coef_re = (num_re * ar + im * ai) / den
    coef_im = (im * ar - num_re * ai) / den
    br, bi = br_ref[...], bi_ref[...]
    bbr = coef_re * br - coef_im * bi
    bbi = coef_re * bi + coef_im * br
    bbr_ref[...] = bbr
    bbi_ref[...] = bbi
    abbr_ref[...] = re * bbr - im * bbi
    abbi_ref[...] = re * bbi + im * bbr


def _ssm_prep(a_re, a_im, log_step, b_re, b_im):
    G, N = a_re.shape
    P = b_re.shape[-1]
    rep = lambda v: jnp.repeat(v, P, axis=0)
    flat = lambda v: jnp.swapaxes(v, 1, 2).reshape(G * P, N)
    ls = log_step.reshape(G, 1)
    small = jax.ShapeDtypeStruct((G, N), F32)
    big = jax.ShapeDtypeStruct((G * P, N), F32)
    return pl.pallas_call(
        _ssm_prep_body,
        out_shape=(small, small, small, small, big, big, big, big),
        name="ssm_discretise",
    )(a_re, a_im, ls, rep(a_re), rep(a_im), rep(ls), flat(b_re), flat(b_im))


def _ssm_block_operands(abr, abi, a2r, a2i, bbr, bbi, abbr, abbi, c_re, c_im):
    G, N = abr.shape
    P = SSM_GROUP
    gb = SSM_GROUPS_PER_BLOCK
    nb = G // gb
    eye = jnp.eye(gb, dtype=F32)

    def b_block(v):
        v = v.reshape(nb, gb, P, 1, N) * eye[None, :, None, :, None]
        return v.reshape(nb, gb * P, gb * N)

    def c_block(v):
        v = jnp.swapaxes(v.reshape(nb, gb, P, N), 2, 3)
        v = v.reshape(nb, gb, N, 1, P) * eye[None, :, None, :, None]
        return v.reshape(nb, gb * N, gb * P)

    bcur = jnp.concatenate([b_block(bbr), b_block(bbi)], axis=-1)
    bprev = jnp.concatenate([b_block(abbr), b_block(abbi)], axis=-1)
    bcat = jnp.concatenate([bcur, bprev], axis=1).astype(BF16)

    c8 = jnp.concatenate([c_block(c_re), -c_block(c_im)], axis=1)
    npair = SSM_BLOCKS_PER_READOUT
    eye2 = jnp.eye(npair, dtype=F32)
    cpair = c8.reshape(nb // npair, npair, 2 * gb * N, 1, gb * P) * eye2[None, :, None, :, None]
    cpair = cpair.reshape(nb // npair, npair * 2 * gb * N, npair * gb * P).astype(BF16)

    first = (lax.broadcasted_iota(jnp.int32, (1, 1, V7X_SUBLANES, 1), 2) < V7X_SUBLANES // 2)
    a1 = jnp.stack([abr, abi], axis=0).reshape(2, nb, 1, gb * N).swapaxes(0, 1)
    a2 = jnp.stack([a2r, a2i], axis=0).reshape(2, nb, 1, gb * N).swapaxes(0, 1)
    a_carry = jnp.where(first, a1, a2)
    return bcat, cpair, a_carry


def _gelu_tanh(x):
    c = math.sqrt(2.0 / math.pi)
    return x * (0.5 * (1.0 + jnp.tanh(c * (x + 0.044715 * (x * x * x)))))


def _ssm_body(x_ref, mod_ref, g_ref, d_ref, bcat_ref, cpair_ref, acar_ref, z_ref,
              hs_scr, hcat_scr, st_scr, y_scr, carry_scr, *, tt, nb):
    i = pl.program_id(0)
    B, _, D = x_ref.shape
    tm = tt * B
    sub = V7X_SUBLANES
    ns = SSM_BLOCK_STATES
    kin = 2 * V7X_LANES
    npair = SSM_BLOCKS_PER_READOUT
    cout = npair * V7X_LANES
    slab = lambda k: slice(k * V7X_LANES, (k + 1) * V7X_LANES)

    @pl.when(i == 0)
    def _():
        carry_scr[...] = jnp.zeros(carry_scr.shape, F32)

    for b in range(B):
        h = _norm_mod(x_ref[b], g_ref[...], mod_ref[b, 3:4, :], mod_ref[b, 4:5, :])
        for k in range(nb):
            hs_scr[k, pl.ds(b, tt, stride=B), :] = h[:, slab(k)]
    second = (lax.broadcasted_iota(jnp.int32, (tm, V7X_LANES), 0) & (sub // 2)) != 0
    for k in range(nb):
        hk = hs_scr[k]
        hcat_scr[:, k * kin:k * kin + V7X_LANES] = hk.astype(BF16)
        hcat_scr[:, k * kin + V7X_LANES:(k + 1) * kin] = jnp.where(
            second, pltpu.roll(hk, sub // 2, 0), 0.0).astype(BF16)

    first = lax.broadcasted_iota(jnp.int32, (sub, ns), 0) < sub // 2
    re_sl, im_sl = slice(0, ns), slice(ns, 2 * ns)

    for sb in range(nb):
        buf = sb % SSM_STATE_BUFFERS
        st_scr[buf] = jnp.dot(hcat_scr[:, sb * kin:(sb + 1) * kin], bcat_ref[sb],
                              preferred_element_type=F32)
        ar, ai = acar_ref[sb, 0], acar_ref[sb, 1]

        def step(r, carry, buf=buf, ar=ar, ai=ai):
            hr, hi = carry
            rows = pl.ds(pl.multiple_of(r * sub, sub), sub)
            sr = st_scr[buf, rows, re_sl] + (hr * ar - hi * ai)
            si = st_scr[buf, rows, im_sl] + (hr * ai + hi * ar)
            st_scr[buf, rows, re_sl] = sr
            st_scr[buf, rows, im_sl] = si
            return (jnp.where(first, pltpu.roll(sr, sub // 2, 0), sr),
                    jnp.where(first, pltpu.roll(si, sub // 2, 0), si))

        hr, hi = lax.fori_loop(0, tm // sub, step, (carry_scr[sb, 0], carry_scr[sb, 1]),
                               unroll=True)
        carry_scr[sb, 0] = hr
        carry_scr[sb, 1] = hi

        if sb % npair == npair - 1:
            pair = sb // npair
            acc = None
            for e in range(npair):
                states = st_scr[(sb - npair + 1 + e) % SSM_STATE_BUFFERS].astype(BF16)
                part = jnp.dot(states, cpair_ref[pair, e * 2 * ns:(e + 1) * 2 * ns, :],
                               preferred_element_type=F32)
                acc = part if acc is None else acc + part
            y_scr[:, pair * cout:(pair + 1) * cout] = acc

    for k in range(nb):
        hs_scr[k] = _gelu_tanh(y_scr[:, slab(k)] + d_ref[:, slab(k)] * hs_scr[k])
    for b in range(B):
        for k in range(nb):
            z_ref[b, :, slab(k)] = hs_scr[k, pl.ds(b, tt, stride=B), :].astype(BF16)


def _ssm(xv, mod, norm_g, d_skip, bcat, cpair, a_carry, *, layer, j, B, S, tt):
    D = norm_g.shape[-1]
    assert V7X_SUBLANES == 2 * B, "the scan packs 2 timesteps x B batches per 8-sublane row-group"
    nb = bcat.shape[0]
    assert nb == D // V7X_LANES
    tt = min(tt, S)
    tm = tt * B
    ns2 = 2 * SSM_BLOCK_STATES
    const = lambda *shape: _resident(shape, lambda i: (0,) * len(shape))
    vmem = 2 * tm * D * 4 + 2 * tm * D * 2 + 2 * tm * D * 4 + 2 * tm * D * 2 \
        + (SSM_STATE_BUFFERS + 2) * tm * ns2 * 4 + (bcat.size + cpair.size) * 2 + 2 * tm * D * 4 \
        + (6 << 20)
    return pl.pallas_call(
        functools.partial(_ssm_body, tt=tt, nb=nb),
        out_shape=jax.ShapeDtypeStruct((B, S, D), BF16),
        grid=(S // tt,),
        in_specs=[
            pl.BlockSpec((B, tt, D), lambda i: (0, i, 0)),
            pl.BlockSpec((None, B, N_SUBLAYERS * 3, D), lambda i: (layer, 0, 0, 0)),
            pl.BlockSpec((None, None, 1, D), lambda i: (layer, 1, 0, 0)),
            pl.BlockSpec((None, 1, D), lambda i: (j, 0, 0)),
            const(*bcat.shape),
            const(*cpair.shape),
            const(*a_carry.shape),
        ],
        out_specs=pl.BlockSpec((B, tt, D), lambda i: (0, i, 0)),
        scratch_shapes=[
            pltpu.VMEM((nb, tm, V7X_LANES), F32),
            pltpu.VMEM((tm, 2 * D), BF16),
            pltpu.VMEM((SSM_STATE_BUFFERS, tm, ns2), F32),
            pltpu.VMEM((tm, D), F32),
            pltpu.VMEM((nb, 2, V7X_SUBLANES, SSM_BLOCK_STATES), F32),
        ],
        compiler_params=_compiler_params(1, vmem),
        name=f"ssm_l{layer}",
    )(xv.reshape(B, S, D), mod, norm_g.reshape(norm_g.shape[0], N_SUBLAYERS, 1, D),
      d_skip.reshape(-1, 1, D), bcat, cpair, a_carry)


def _glu_body(x_ref, z_ref, wv_ref, wg_ref, mod_ref, o_ref):
    z = z_ref[...]
    val = jnp.dot(z, wv_ref[...], preferred_element_type=F32)
    gt = jnp.dot(z, wg_ref[...], preferred_element_type=F32)
    o_ref[...] = x_ref[...] + mod_ref[5:6, :] * (val * jax.nn.sigmoid(gt))


def _glu(xv, zv, w, mod, *, layer, j, B, S, tm, tn):
    D = w.shape[1]
    tm = min(tm, S)
    nt, nn = S // tm, D // tn
    vmem = 4 * tm * tn * 4 + 2 * tm * D * 2 + 4 * D * tn * 2 + 3 * tm * tn * 4 + (6 << 20)
    return pl.pallas_call(
        _glu_body,
        out_shape=jax.ShapeDtypeStruct((B * S, D), F32),
        grid=(nn, B, nt),
        in_specs=[
            pl.BlockSpec((tm, tn), lambda jn, b, ti: (b * nt + ti, jn)),
            pl.BlockSpec((tm, D), lambda jn, b, ti: (b * nt + ti, 0)),
            pl.BlockSpec((None, D, tn), lambda jn, b, ti: (j, 0, jn)),
            pl.BlockSpec((None, D, tn), lambda jn, b, ti: (j, 0, nn + jn)),
            pl.BlockSpec((None, None, N_SUBLAYERS * 3, tn), lambda jn, b, ti: (layer, b, 0, jn)),
        ],
        out_specs=pl.BlockSpec((tm, tn), lambda jn, b, ti: (b * nt + ti, jn)),
        compiler_params=_compiler_params(3, vmem),
        name=f"glu_l{layer}",
    )(xv, zv, w, w, mod)


def kernel(x, c, positions, norm_g, ada_w, ada_b, ffn_w_in, ffn_w_out, attn_w_in, attn_w_out,
           attn_q_norm, attn_k_norm, attn_lambda, attn_subln, ssm_a_re, ssm_a_im, ssm_log_step,
           ssm_b_re, ssm_b_im, ssm_c_re, ssm_c_im, ssm_d, ssm_w_glu):
    B, S, D = x.shape
    depth = norm_g.shape[0]

    mod = _adaln(c, ada_w, ada_b)
    attn_w_in = attn_w_in.astype(BF16)
    attn_w_out = attn_w_out.astype(BF16)
    ssm_w_glu = ssm_w_glu.astype(BF16)
    pos = positions.reshape(B * S, 1)

    ffn_order = [(i, which) for i in range(depth) for which in range(2)]
    ffn_w = (ffn_w_in[0, 0].astype(BF16), ffn_w_out[0, 0].astype(BF16))

    def ffn(xv, ffn_w, n):
        layer, which = ffn_order[n]
        nxt = (ffn_w_in, ffn_w_out) + ffn_order[n + 1] if n + 1 < len(ffn_order) else None
        return _ffn(xv, mod, norm_g, *ffn_w, nxt, layer=layer, sl=2 * which, B=B, S=S,
                    tm=FFN_ROWS, tf=FFN_F_TILE)

    xv = x.reshape(B * S, D)
    for i in range(depth):
        j = i // N_MIXERS
        xv, ffn_w = ffn(xv, ffn_w, 2 * i)
        if i % N_MIXERS == 0:
            lambda_init = 0.8 - 0.6 * math.exp(-0.3 * i)
            qkv = _qkv(xv, pos, mod, norm_g, attn_w_in, attn_q_norm, attn_k_norm,
                       layer=i, j=j, B=B, S=S, tm=QKV_ROWS, tn=QKV_COL_CHUNK)
            att = _attn(qkv, attn_lambda, attn_subln, j=j, B=B, S=S, D=D, tq=ATTN_QUERY_ROWS,
                        lambda_init=lambda_init)
            xv = _oproj(xv, att, attn_w_out, mod, layer=i, j=j, B=B, S=S, tm=OPROJ_ROWS)
        else:
            disc = _ssm_prep(ssm_a_re[j], ssm_a_im[j], ssm_log_step[j], ssm_b_re[j], ssm_b_im[j])
            operands = _ssm_block_operands(*disc, ssm_c_re[j], ssm_c_im[j])
            z = _ssm(xv, mod, norm_g, ssm_d, *operands, layer=i, j=j, B=B, S=S,
                     tt=SSM_TIMESTEPS)
            xv = _glu(xv, z.reshape(B * S, D), ssm_w_glu, mod, layer=i, j=j, B=B, S=S,
                      tm=GLU_ROWS, tn=GLU_COL_TILE)
        xv, ffn_w = ffn(xv, ffn_w, 2 * i + 1)
    return xv.reshape(B, S, D)
```
